```python
import jax
import jax.numpy as jnp
from jax import lax
import numpy as np


D_MODEL = 1024
BATCH = 1
SEQ = 16384
DEPTH = 2
DEC_BATCH = 4
DEC_SEQ = 4096
PAST_LEN = 128

GRID_W = 64
D_CONV = 512
CONV_W = 31
N_HEADS = 8
HEAD_DIM = 64
D_ATTN = N_HEADS * HEAD_DIM
MAX_WIN_H = 8
WIN_W = 16
D_FF = 2816
N_EXPERTS = 8
TOP_K = 2
D_FF_EXPERT = 2816
N_DENSE = (DEPTH + 1) // 2
N_MOE = DEPTH // 2
D_IN = 2 * D_CONV + 3 * D_ATTN + 2 * D_MODEL
EPS = 1e-6

kernel_name = 'hybrid_conv_natten_moe_encoder'


def rmsnorm(x, g):
    xf = x.astype(jnp.float32)
    y = xf * lax.rsqrt(jnp.mean(xf * xf, axis=-1, keepdims=True) + EPS)
    return (y * g.astype(jnp.float32)).astype(x.dtype)


def layernorm(x, g, b):
    xf = x.astype(jnp.float32)
    xc = xf - jnp.mean(xf, axis=-1, keepdims=True)
    y = xc * lax.rsqrt(jnp.mean(xc * xc, axis=-1, keepdims=True) + EPS)
    return (y * g.astype(jnp.float32) + b.astype(jnp.float32)).astype(x.dtype)


def conv_module(u, conv_w, conv_b, ln_g, ln_b):
    a, gate = jnp.split(u, 2, axis=-1)
    v = a * jax.nn.sigmoid(gate)
    pad = CONV_W // 2
    y = lax.conv_general_dilated(
        v, conv_w[:, None, :], window_strides=(1,), padding=[(pad, pad)],
        dimension_numbers=('NWC', 'WIO', 'NWC'), feature_group_count=D_CONV)
    y = layernorm(y + conv_b, ln_g, ln_b)
    return jax.nn.silu(y)


def neighbourhood_attention(q, k, v, rpb):
    B, S, H, Dh = q.shape
    rows = S // GRID_W
    win_h = min(MAX_WIN_H, rows)
    qg = q.reshape(B, rows, GRID_W, H, Dh)
    kg = k.reshape(B, rows, GRID_W, H, Dh)
    vg = v.reshape(B, rows, GRID_W, H, Dh)
    qrow = jnp.arange(rows)
    start_r = jnp.clip(qrow - win_h // 2, 0, rows - win_h)
    row_idx = start_r[:, None] + jnp.arange(win_h)[None, :]
    k_rows = kg[:, row_idx]
    v_rows = vg[:, row_idx]
    s = jnp.einsum('brqhd,brwkhd->bhrqwk', qg, k_rows,
                   preferred_element_type=jnp.float32) * (Dh ** -0.5)
    qcol = jnp.arange(GRID_W)
    kcol = jnp.arange(GRID_W)
    start_c = jnp.clip(qcol - WIN_W // 2, 0, GRID_W - WIN_W)
    col_mask = (kcol[None, :] >= start_c[:, None]) & (kcol[None, :] < start_c[:, None] + WIN_W)
    ro_i = row_idx - qrow[:, None] + (MAX_WIN_H - 1)
    co_i = jnp.clip(kcol[None, :] - qcol[:, None] + (WIN_W - 1), 0, 2 * WIN_W - 2)
    bias = rpb.astype(jnp.float32)[:, ro_i[:, None, :, None], co_i[None, :, None, :]]
    s = jnp.where(col_mask[:, None, :], s + bias[None], -jnp.inf)
    p = jax.nn.softmax(s, axis=(-2, -1)).astype(v.dtype)
    o = jnp.einsum('bhrqwk,brwkhd->brqhd', p, v_rows)
    return o.reshape(B, S, H * Dh)


def mixer_sublayer(x, norm_g, w_in, b_gate, conv_w, conv_b, ln_g, ln_b,
                   q_norm_g, k_norm_g, rpb, w_proj_a, w_proj_b, w_out):
    B, S, _ = x.shape
    h = rmsnorm(x, norm_g)
    proj = h @ w_in
    u_conv = proj[..., :2 * D_CONV]
    qkv = proj[..., 2 * D_CONV:2 * D_CONV + 3 * D_ATTN]
    gates = jax.nn.sigmoid(proj[..., 2 * D_CONV + 3 * D_ATTN:] + b_gate)
    g_a, g_b = jnp.split(gates, 2, axis=-1)
    br_a = conv_module(u_conv, conv_w, conv_b, ln_g, ln_b) @ w_proj_a
    q, k, v = jnp.split(qkv, 3, axis=-1)
    q = rmsnorm(q.reshape(B, S, N_HEADS, HEAD_DIM), q_norm_g)
    k = rmsnorm(k.reshape(B, S, N_HEADS, HEAD_DIM), k_norm_g)
    v = v.reshape(B, S, N_HEADS, HEAD_DIM)
    br_b = neighbourhood_attention(q, k, v, rpb) @ w_proj_b
    return x + (g_a * br_a + g_b * br_b) @ w_out


def swiglu(h, w1, w3, w2):
    return (jax.nn.silu(h @ w1) * (h @ w3)) @ w2


def moe_swiglu(h, router_w, router_b, w1, w3, w2):
    logits = (h @ router_w).astype(jnp.float32) + router_b.astype(jnp.float32)
    top_v, top_i = lax.top_k(logits, TOP_K)
    top_w = jax.nn.softmax(top_v, axis=-1)
    gate = jnp.sum(jax.nn.one_hot(top_i, N_EXPERTS, dtype=jnp.float32) * top_w[..., None], axis=-2)
    gate = gate.astype(h.dtype)
    out = jnp.zeros_like(h)
    for e in range(N_EXPERTS):
        out = out + gate[..., e:e + 1] * swiglu(h, w1[e], w3[e], w2[e])
    return out


def trunk(x, norm1_g, w_in, b_gate, conv_w, conv_b, conv_ln_g, conv_ln_b, q_norm_g, k_norm_g,
          rpb, w_proj_a, w_proj_b, w_out, norm2_g, ffn_w1, ffn_w3, ffn_w2,
          router_w, router_b, moe_w1, moe_w3, moe_w2):
    for l in range(DEPTH):
        x = mixer_sublayer(x, norm1_g[l], w_in[l], b_gate[l], conv_w[l], conv_b[l],
                           conv_ln_g[l], conv_ln_b[l], q_norm_g[l], k_norm_g[l], rpb[l],
                           w_proj_a[l], w_proj_b[l], w_out[l])
        h = rmsnorm(x, norm2_g[l])
        j = l // 2
        if l % 2 == 0:
            x = x + swiglu(h, ffn_w1[j], ffn_w3[j], ffn_w2[j])
        else:
            x = x + moe_swiglu(h, router_w[j], router_b[j], moe_w1[j], moe_w3[j], moe_w2[j])
    return x


def setup_inputs(seed: int = 0) -> dict:
    key = jax.random.key(seed)
    ks = jax.random.split(key, 24)
    f32 = jnp.float32
    n = lambda k, shape, scale: jax.random.normal(k, shape, f32) * scale
    return {
        'x_prompt': n(ks[0], (BATCH, SEQ, D_MODEL), 1.0),
        'x_sample': n(ks[1], (DEC_BATCH, DEC_SEQ, D_MODEL), 1.0),
        'norm1_g': 1.0 + n(ks[2], (DEPTH, D_MODEL), 0.01),
        'w_in': n(ks[3], (DEPTH, D_MODEL, D_IN), D_MODEL ** -0.5),
        'b_gate': n(ks[4], (DEPTH, 2 * D_MODEL), 0.01),
        'conv_w': n(ks[5], (DEPTH, CONV_W, D_CONV), CONV_W ** -0.5),
        'conv_b': n(ks[6], (DEPTH, D_CONV), 0.01),
        'conv_ln_g': 1.0 + n(ks[7], (DEPTH, D_CONV), 0.01),
        'conv_ln_b': n(ks[8], (DEPTH, D_CONV), 0.01),
        'q_norm_g': 1.0 + n(ks[9], (DEPTH, HEAD_DIM), 0.01),
        'k_norm_g': 1.0 + n(ks[10], (DEPTH, HEAD_DIM), 0.01),
        'rpb': n(ks[11], (DEPTH, N_HEADS, 2 * MAX_WIN_H - 1, 2 * WIN_W - 1), 0.02),
        'w_proj_a': n(ks[12], (DEPTH, D_CONV, D_MODEL), D_CONV ** -0.5),
        'w_proj_b': n(ks[13], (DEPTH, D_ATTN, D_MODEL), D_ATTN ** -0.5),
        'w_out': n(ks[14], (DEPTH, D_MODEL, D_MODEL), D_MODEL ** -0.5),
        'norm2_g': 1.0 + n(ks[15], (DEPTH, D_MODEL), 0.01),
        'ffn_w1': n(ks[16], (N_DENSE, D_MODEL, D_FF), D_MODEL ** -0.5),
        'ffn_w3': n(ks[17], (N_DENSE, D_MODEL, D_FF), D_MODEL ** -0.5),
        'ffn_w2': n(ks[18], (N_DENSE, D_FF, D_MODEL), D_FF ** -0.5),
        'router_w': n(ks[19], (N_MOE, D_MODEL, N_EXPERTS), D_MODEL ** -0.5),
        'router_b': n(ks[20], (N_MOE, N_EXPERTS), 0.01),
        'moe_w1': n(ks[21], (N_MOE, N_EXPERTS, D_MODEL, D_FF_EXPERT), D_MODEL ** -0.5),
        'moe_w3': n(ks[22], (N_MOE, N_EXPERTS, D_MODEL, D_FF_EXPERT), D_MODEL ** -0.5),
        'moe_w2': n(ks[23], (N_MOE, N_EXPERTS, D_FF_EXPERT, D_MODEL), D_FF_EXPERT ** -0.5),
    }


def reference(x_prompt, x_sample, norm1_g, w_in, b_gate, conv_w, conv_b, conv_ln_g, conv_ln_b,
              q_norm_g, k_norm_g, rpb, w_proj_a, w_proj_b, w_out, norm2_g, ffn_w1, ffn_w3, ffn_w2,
              router_w, router_b, moe_w1, moe_w3, moe_w2):
    y_prompt = trunk(x_prompt, norm1_g, w_in, b_gate, conv_w, conv_b, conv_ln_g, conv_ln_b,
                     q_norm_g, k_norm_g, rpb, w_proj_a, w_proj_b, w_out, norm2_g,
                     ffn_w1, ffn_w3, ffn_w2, router_w, router_b, moe_w1, moe_w3, moe_w2)
    y_sample = trunk(x_sample, norm1_g, w_in, b_gate, conv_w, conv_b, conv_ln_g, conv_ln_b,
                     q_norm_g, k_norm_g, rpb, w_proj_a, w_proj_b, w_out, norm2_g,
                     ffn_w1, ffn_w3, ffn_w2, router_w, router_b, moe_w1, moe_w3, moe_w2)
    return (y_prompt, y_sample)
```

```python
import functools

import jax
import jax.numpy as jnp
from jax import lax
from jax.experimental import pallas as pl
from jax.experimental.pallas import tpu as pltpu

D_MODEL = 1024
GRID_W = 64
D_CONV = 512
CONV_W = 31
N_HEADS = 8
HEAD_DIM = 64
D_ATTN = N_HEADS * HEAD_DIM
MAX_WIN_H = 8
WIN_W = 16
D_FF = 2816
N_EXPERTS = 8
EPS = 1e-6

LANES = 128
TM = 512
TILE_ROWS = TM // GRID_W
KV_HALO = 256
CONV_HALO = 16
CONV_PAD = CONV_W // 2
N_PAIR_ROWS = 2 * MAX_WIN_H - 2
NEG_BIG = -1e30
FF_CHUNK = 256
VMEM_LIMIT = 56 * 1024 * 1024

F32 = jnp.float32
BF16 = jnp.bfloat16


def _sigmoid(x):
    return 1.0 / (1.0 + jnp.exp(-x))


def _bias_table_kernel(rpb_ref, out_ref):
    ro = pl.program_id(0)
    qcol = lax.broadcasted_iota(jnp.int32, (GRID_W, LANES), 0)
    lane = lax.broadcasted_iota(jnp.int32, (GRID_W, LANES), 1)
    second = lane >= GRID_W
    kcol = jnp.where(second, lane - GRID_W, lane)
    start_c = jnp.clip(qcol - WIN_W // 2, 0, GRID_W - WIN_W)
    valid = (kcol >= start_c) & (kcol < start_c + WIN_W)
    co = jnp.clip(kcol - qcol + (WIN_W - 1), 0, 2 * WIN_W - 2)
    n_co = 2 * WIN_W - 1
    n_ro = 2 * MAX_WIN_H - 1
    for h in range(N_HEADS):
        acc = jnp.zeros((GRID_W, LANES), F32)
        for d in range(n_co):
            v0 = rpb_ref[(h * n_ro + ro) * n_co + d]
            v1 = rpb_ref[(h * n_ro + ro + 1) * n_co + d]
            acc = jnp.where(co == d, jnp.where(second, v1, v0), acc)
        out_ref[0, h] = jnp.where(valid, acc, NEG_BIG)


def _bias_table(rpb_l):
    return pl.pallas_call(
        _bias_table_kernel,
        grid=(N_PAIR_ROWS,),
        in_specs=[pl.BlockSpec(memory_space=pltpu.SMEM)],
        out_specs=pl.BlockSpec((1, N_HEADS, GRID_W, LANES), lambda r: (r, 0, 0, 0)),
        out_shape=jax.ShapeDtypeStruct((N_PAIR_ROWS, N_HEADS, GRID_W, LANES), F32),
        name="bias_table",
    )(rpb_l.reshape(-1))


def _in_proj_kernel(x_ref, g_ref, w_ref, bg_ref, qg_ref, kg_ref, hm_ref,
                    vglu_ref, q_ref, k_ref, v_ref, ga_ref, gb_ref):
    x = x_ref[...]
    ms = jnp.mean(x * x, axis=-1, keepdims=True)
    h = (x * lax.rsqrt(ms + EPS) * g_ref[...]).astype(BF16)

    def proj(lo, hi):
        return jnp.dot(h, w_ref[:, lo:hi], preferred_element_type=F32)

    def head_norm(t, gain):
        msq = jnp.dot((t * t).astype(BF16), hm_ref[...], preferred_element_type=F32)
        return t * lax.rsqrt(msq + EPS) * gain

    a = proj(0, D_CONV)
    gate = proj(D_CONV, 2 * D_CONV)
    vglu_ref[...] = (a * _sigmoid(gate)).astype(BF16)
    o = 2 * D_CONV
    q_ref[...] = (head_norm(proj(o, o + D_ATTN), qg_ref[...]) * (HEAD_DIM ** -0.5)).astype(BF16)
    k_ref[...] = head_norm(proj(o + D_ATTN, o + 2 * D_ATTN), kg_ref[...]).astype(BF16)
    v_ref[...] = proj(o + 2 * D_ATTN, o + 3 * D_ATTN).astype(BF16)
    o += 3 * D_ATTN
    bg = bg_ref[...]
    ga_ref[...] = _sigmoid(proj(o, o + D_MODEL) + bg[:, :D_MODEL]).astype(BF16)
    gb_ref[...] = _sigmoid(proj(o + D_MODEL, o + 2 * D_MODEL) + bg[:, D_MODEL:]).astype(BF16)


def _in_proj(x, norm_g, w_in, b_gate, q_norm_g, k_norm_g, head_mean):
    n_tok = x.shape[0]
    d_in = w_in.shape[1]
    full = lambda shape: pl.BlockSpec(shape, lambda i: (0, 0))
    tile = lambda width: pl.BlockSpec((TM, width), lambda i: (i, 0))
    out = lambda width: jax.ShapeDtypeStruct((n_tok, width), BF16)
    return pl.pallas_call(
        _in_proj_kernel,
        grid=(n_tok // TM,),
        in_specs=[tile(D_MODEL), full((1, D_MODEL)), full((D_MODEL, d_in)),
                  full((1, 2 * D_MODEL)), full((1, D_ATTN)), full((1, D_ATTN)),
                  full((D_ATTN, D_ATTN))],
        out_specs=[tile(D_CONV), tile(D_ATTN), tile(D_ATTN), tile(D_ATTN),
                   tile(D_MODEL), tile(D_MODEL)],
        out_shape=[out(D_CONV), out(D_ATTN), out(D_ATTN), out(D_ATTN),
                   out(D_MODEL), out(D_MODEL)],
        compiler_params=pltpu.CompilerParams(
            dimension_semantics=("arbitrary",), vmem_limit_bytes=VMEM_LIMIT),
        name="in_proj",
    )(x, norm_g, w_in, b_gate, q_norm_g, k_norm_g, head_mean)


def _mixer_kernel(first_ref, last_ref,
                  x_ref, vg_ref, vgp_ref, vgn_ref, q_ref,
                  k_ref, kp_ref, kn_ref, v_ref, vp_ref, vn_ref,
                  ga_ref, gb_ref, cw_ref, cb_ref, lg_ref, lb_ref, tab_ref,
                  wa_ref, wb_ref, wo_ref, n2_ref, rw_ref, rb_ref,
                  xo_ref, h2_ref, gate_ref,
                  ext_ref, sh_ref, conv_ref, kext_ref, vext_ref, att_ref,
                  *, with_router):
    i = pl.program_id(0)
    is_first = first_ref[i]
    is_last = last_ref[i]

    keep_prev = (1 - is_first).astype(F32)
    keep_next = (1 - is_last).astype(F32)
    ext_ref[0:CONV_HALO, :] = vgp_ref[...].astype(F32) * keep_prev
    ext_ref[CONV_HALO:CONV_HALO + TM, :] = vg_ref[...].astype(F32)
    ext_ref[CONV_HALO + TM:, :] = vgn_ref[...].astype(F32) * keep_next
    n_sh_rows = sh_ref.shape[1]
    row_chunk = 128
    for cb in range(D_CONV // LANES):
        cols = slice(cb * LANES, (cb + 1) * LANES)
        for b in range(8):
            sh_ref[b] = ext_ref[b:b + n_sh_rows, cols]
        for rc in range(TM // row_chunk):
            acc = jnp.zeros((row_chunk, LANES), F32)
            for j in range(CONV_W):
                off = CONV_HALO - CONV_PAD + j
                a8, b = off // 8, off % 8
                start = rc * row_chunk + 8 * a8
                acc = acc + sh_ref[b, start:start + row_chunk, :] * cw_ref[j:j + 1, cols]
            conv_ref[rc * row_chunk:(rc + 1) * row_chunk, cols] = acc
    y = conv_ref[...] + cb_ref[...]
    mu = jnp.mean(y, axis=-1, keepdims=True)
    yc = y - mu
    var = jnp.mean(yc * yc, axis=-1, keepdims=True)
    yn = yc * lax.rsqrt(var + EPS) * lg_ref[...] + lb_ref[...]
    act = (yn * _sigmoid(yn)).astype(BF16)
    br_a = jnp.dot(act, wa_ref[...], preferred_element_type=F32)

    kext_ref[0:KV_HALO, :] = kp_ref[...]
    kext_ref[KV_HALO:KV_HALO + TM, :] = k_ref[...]
    kext_ref[KV_HALO + TM:, :] = kn_ref[...]
    vext_ref[0:KV_HALO, :] = vp_ref[...]
    vext_ref[KV_HALO:KV_HALO + TM, :] = v_ref[...]
    vext_ref[KV_HALO + TM:, :] = vn_ref[...]
    lane = lax.broadcasted_iota(jnp.int32, (GRID_W, LANES), 1)
    low_half = lane < HEAD_DIM
    win = MAX_WIN_H * GRID_W

    def row_body(j, carry):
        ws = j - MAX_WIN_H // 2
        ws = jnp.where(is_first == 1, jnp.maximum(ws, 0), ws)
        ws = jnp.where(is_last == 1, jnp.minimum(ws, 0), ws)
        ro0 = ws - j + (MAX_WIN_H - 1)
        kstart = pl.multiple_of((ws + KV_HALO // GRID_W) * GRID_W, GRID_W)
        qstart = pl.multiple_of(j * GRID_W, GRID_W)
        for p in range(N_HEADS // 2):
            cols = slice(p * LANES, (p + 1) * LANES)
            q2 = q_ref[pl.ds(qstart, GRID_W), cols]
            kslab = kext_ref[pl.ds(kstart, win), cols]
            vslab = vext_ref[pl.ds(kstart, win), cols]
            outs = []
            for hh in range(2):
                h = 2 * p + hh
                qa = jnp.where(low_half if hh == 0 else ~low_half, q2, jnp.zeros_like(q2))
                s = lax.dot_general(qa, kslab, (((1,), (1,)), ((), ())),
                                    preferred_element_type=F32)
                bias = jnp.concatenate(
                    [tab_ref[ro0 + 2 * c, h] for c in range(MAX_WIN_H // 2)], axis=1)
                s = s + bias
                m = jnp.max(s, axis=1, keepdims=True)
                pexp = jnp.exp(s - m)
                denom = jnp.sum(pexp, axis=1, keepdims=True)
                o = jnp.dot(pexp.astype(BF16), vslab, preferred_element_type=F32)
                outs.append(o / denom)
            att_ref[pl.ds(qstart, GRID_W), cols] = jnp.where(low_half, outs[0], outs[1])
        return carry

    lax.fori_loop(0, TILE_ROWS, row_body, 0)
    br_b = jnp.dot(att_ref[...].astype(BF16), wb_ref[...], preferred_element_type=F32)

    merged = ga_ref[...].astype(F32) * br_a + gb_ref[...].astype(F32) * br_b
    x_new = x_ref[...] + jnp.dot(merged.astype(BF16), wo_ref[...], preferred_element_type=F32)
    xo_ref[...] = x_new
    ms = jnp.mean(x_new * x_new, axis=-1, keepdims=True)
    h2 = (x_new * lax.rsqrt(ms + EPS) * n2_ref[...]).astype(BF16)
    h2_ref[...] = h2

    if with_router:
        logits = jnp.dot(h2, rw_ref[...], preferred_element_type=F32) + rb_ref[...]
        elane = lax.broadcasted_iota(jnp.int32, logits.shape, 1)
        real = elane < N_EXPERTS
        logits = jnp.where(real, logits, -jnp.inf)
        m1 = jnp.max(logits, axis=1, keepdims=True)
        i1 = jnp.min(jnp.where(logits == m1, elane, LANES), axis=1, keepdims=True)
        rest = jnp.where(elane == i1, -jnp.inf, logits)
        m2 = jnp.max(rest, axis=1, keepdims=True)
        i2 = jnp.min(jnp.where(rest == m2, elane, LANES), axis=1, keepdims=True)
        e2 = jnp.exp(m2 - m1)
        w1 = 1.0 / (1.0 + e2)
        w2 = e2 / (1.0 + e2)
        gate_ref[...] = jnp.where(elane == i1, w1, jnp.where(elane == i2, w2, 0.0))
    else:
        gate_ref[...] = jnp.zeros(gate_ref.shape, F32)


def _mixer(x, vglu, q, k, v, ga, gb, first, last, conv_w, conv_b, ln_g, ln_b, table,
           w_a, w_b, w_o, norm2_g, router_w, router_b, *, with_router):
    n_tok = x.shape[0]
    conv_blocks = TM // CONV_HALO
    kv_blocks = TM // KV_HALO
    full2 = lambda shape: pl.BlockSpec(shape, lambda i, f, l: (0, 0))
    tile = lambda width: pl.BlockSpec((TM, width), lambda i, f, l: (i, 0))
    in_specs = [
        tile(D_MODEL),
        tile(D_CONV),
        pl.BlockSpec((CONV_HALO, D_CONV), lambda i, f, l: (i * conv_blocks - 1 + f[i], 0)),
        pl.BlockSpec((CONV_HALO, D_CONV), lambda i, f, l: ((i + 1) * conv_blocks - l[i], 0)),
        tile(D_ATTN),
        tile(D_ATTN),
        pl.BlockSpec((KV_HALO, D_ATTN), lambda i, f, l: (i * kv_blocks - 1 + f[i], 0)),
        pl.BlockSpec((KV_HALO, D_ATTN), lambda i, f, l: ((i + 1) * kv_blocks - l[i], 0)),
        tile(D_ATTN),
        pl.BlockSpec((KV_HALO, D_ATTN), lambda i, f, l: (i * kv_blocks - 1 + f[i], 0)),
        pl.BlockSpec((KV_HALO, D_ATTN), lambda i, f, l: ((i + 1) * kv_blocks - l[i], 0)),
        tile(D_MODEL),
        tile(D_MODEL),
        full2((CONV_W + 1, D_CONV)),
        full2((1, D_CONV)), full2((1, D_CONV)), full2((1, D_CONV)),
        pl.BlockSpec((N_PAIR_ROWS, N_HEADS, GRID_W, LANES), lambda i, f, l: (0, 0, 0, 0)),
        full2((D_CONV, D_MODEL)), full2((D_ATTN, D_MODEL)), full2((D_MODEL, D_MODEL)),
        full2((1, D_MODEL)),
        full2((D_MODEL, LANES)), full2((1, LANES)),
    ]
    out_specs = [tile(D_MODEL), tile(D_MODEL), tile(LANES)]
    out_shape = [jax.ShapeDtypeStruct((n_tok, D_MODEL), F32),
                 jax.ShapeDtypeStruct((n_tok, D_MODEL), BF16),
                 jax.ShapeDtypeStruct((n_tok, LANES), F32)]
    ext_rows = TM + 2 * CONV_HALO
    scratch = [
        pltpu.VMEM((ext_rows, D_CONV), F32),
        pltpu.VMEM((8, ext_rows - 8, LANES), F32),
        pltpu.VMEM((TM, D_CONV), F32),
        pltpu.VMEM((TM + 2 * KV_HALO, D_ATTN), BF16),
        pltpu.VMEM((TM + 2 * KV_HALO, D_ATTN), BF16),
        pltpu.VMEM((TM, D_ATTN), F32),
    ]
    return pl.pallas_call(
        functools.partial(_mixer_kernel, with_router=with_router),
        grid_spec=pltpu.PrefetchScalarGridSpec(
            num_scalar_prefetch=2, grid=(n_tok // TM,),
            in_specs=in_specs, out_specs=out_specs, scratch_shapes=scratch),
        out_shape=out_shape,
        compiler_params=pltpu.CompilerParams(
            dimension_semantics=("arbitrary",), vmem_limit_bytes=VMEM_LIMIT),
        name="mixer",
    )(first, last, x, vglu, vglu, vglu, q, k, k, k, v, v, v, ga, gb,
      conv_w, conv_b, ln_g, ln_b, table, w_a, w_b, w_o, norm2_g, router_w, router_b)


def _ffn_kernel(h_ref, x_ref, w1_ref, w3_ref, w2_ref, o_ref):
    h = h_ref[...]
    acc = x_ref[...]
    for c in range(0, D_FF, FF_CHUNK):
        a = jnp.dot(h, w1_ref[:, c:c + FF_CHUNK], preferred_element_type=F32)
        b = jnp.dot(h, w3_ref[:, c:c + FF_CHUNK], preferred_element_type=F32)
        act = (a * _sigmoid(a) * b).astype(BF16)
        acc = acc + jnp.dot(act, w2_ref[c:c + FF_CHUNK, :], preferred_element_type=F32)
    o_ref[...] = acc


def _ffn(h2, x, w1, w3, w2):
    n_tok = x.shape[0]
    tile = lambda: pl.BlockSpec((TM, D_MODEL), lambda i: (i, 0))
    single = dict(pipeline_mode=pl.Buffered(1))
    return pl.pallas_call(
        _ffn_kernel,
        grid=(n_tok // TM,),
        in_specs=[tile(), tile(),
                  pl.BlockSpec((D_MODEL, D_FF), lambda i: (0, 0), **single),
                  pl.BlockSpec((D_MODEL, D_FF), lambda i: (0, 0), **single),
                  pl.BlockSpec((D_FF, D_MODEL), lambda i: (0, 0), **single)],
        out_specs=tile(),
        out_shape=jax.ShapeDtypeStruct((n_tok, D_MODEL), F32),
        compiler_params=pltpu.CompilerParams(
            dimension_semantics=("arbitrary",), vmem_limit_bytes=VMEM_LIMIT),
        name="ffn",
    )(h2, x, w1, w3, w2)


MOE_TM = 1024
MOE_FC = 1408


def _moe_dense_kernel(h_ref, x_ref, gate_ref, w1_ref, w3_ref, w2_ref, o_ref):
    e = pl.program_id(1)
    f = pl.program_id(2)

    @pl.when((e == 0) & (f == 0))
    def _():
        o_ref[...] = x_ref[...]

    h = h_ref[...]
    a = jnp.dot(h, w1_ref[0], preferred_element_type=F32)
    b = jnp.dot(h, w3_ref[0], preferred_element_type=F32)
    lane = lax.broadcasted_iota(jnp.int32, gate_ref.shape, 1)
    g = jnp.sum(jnp.where(lane == e, gate_ref[...], 0.0), axis=1, keepdims=True)
    act = (a * _sigmoid(a) * b).astype(BF16)
    o_ref[...] += g * jnp.dot(act, w2_ref[0], preferred_element_type=F32)


def _moe_dense(h2, x, gate, w1, w3, w2):
    n_tok = x.shape[0]
    tile = lambda width: pl.BlockSpec((MOE_TM, width), lambda i, e, f: (i, 0))
    return pl.pallas_call(
        _moe_dense_kernel,
        grid=(n_tok // MOE_TM, N_EXPERTS, D_FF // MOE_FC),
        in_specs=[tile(D_MODEL), tile(D_MODEL), tile(LANES),
                  pl.BlockSpec((1, D_MODEL, MOE_FC), lambda i, e, f: (e, 0, f)),
                  pl.BlockSpec((1, D_MODEL, MOE_FC), lambda i, e, f: (e, 0, f)),
                  pl.BlockSpec((1, MOE_FC, D_MODEL), lambda i, e, f: (e, f, 0))],
        out_specs=tile(D_MODEL),
        out_shape=jax.ShapeDtypeStruct((n_tok, D_MODEL), F32),
        compiler_params=pltpu.CompilerParams(
            dimension_semantics=("arbitrary", "arbitrary", "arbitrary"),
            vmem_limit_bytes=VMEM_LIMIT),
        name="moe_dense",
    )(h2, x, gate, w1, w3, w2)


def _tile_edges(seq_lens):
    first, last = [], []
    for n in seq_lens:
        t = n // TM
        first += [1] + [0] * (t - 1)
        last += [0] * (t - 1) + [1]
    return jnp.asarray(first, jnp.int32), jnp.asarray(last, jnp.int32)


def kernel(x_prompt, x_sample, norm1_g, w_in, b_gate, conv_w, conv_b, conv_ln_g, conv_ln_b,
           q_norm_g, k_norm_g, rpb, w_proj_a, w_proj_b, w_out, norm2_g, ffn_w1, ffn_w3, ffn_w2,
           router_w, router_b, moe_w1, moe_w3, moe_w2):
    depth = w_in.shape[0]
    seq_lens = [x_prompt.shape[1]] * x_prompt.shape[0] + [x_sample.shape[1]] * x_sample.shape[0]
    assert all(n % TM == 0 and n // GRID_W >= 2 * TILE_ROWS for n in seq_lens)
    first, last = _tile_edges(seq_lens)
    n_prompt = x_prompt.shape[0] * x_prompt.shape[1]
    x = jnp.concatenate([x_prompt.reshape(-1, D_MODEL), x_sample.reshape(-1, D_MODEL)], axis=0)

    head_id = jnp.arange(D_ATTN) // HEAD_DIM
    head_mean = jnp.where(head_id[:, None] == head_id[None, :], 1.0 / HEAD_DIM, 0.0).astype(BF16)
    row = lambda a: a.reshape(1, -1).astype(F32)

    for l in range(depth):
        table = _bias_table(rpb[l])
        vglu, q, k, v, ga, gb = _in_proj(
            x, row(norm1_g[l]), w_in[l].astype(BF16), row(b_gate[l]),
            row(jnp.tile(q_norm_g[l], N_HEADS)), row(jnp.tile(k_norm_g[l], N_HEADS)), head_mean)
        j = l // 2
        moe = l % 2 == 1
        if moe:
            rw = jnp.pad(router_w[j], ((0, 0), (0, LANES - N_EXPERTS))).astype(BF16)
            rb = jnp.pad(row(router_b[j]), ((0, 0), (0, LANES - N_EXPERTS)))
        else:
            rw = jnp.zeros((D_MODEL, LANES), BF16)
            rb = jnp.zeros((1, LANES), F32)
        cw = jnp.pad(conv_w[l], ((0, 1), (0, 0)))
        x, h2, gate = _mixer(
            x, vglu, q, k, v, ga, gb, first, last, cw, row(conv_b[l]), row(conv_ln_g[l]),
            row(conv_ln_b[l]), table, w_proj_a[l].astype(BF16), w_proj_b[l].astype(BF16),
            w_out[l].astype(BF16), row(norm2_g[l]), rw, rb, with_router=moe)
        if moe:
            x = _moe_dense(h2, x, gate, moe_w1[j].astype(BF16), moe_w3[j].astype(BF16),
                           moe_w2[j].astype(BF16))
        else:
            x = _ffn(h2, x, ffn_w1[j].astype(BF16), ffn_w3[j].astype(BF16),
                     ffn_w2[j].astype(BF16))

    y_prompt = x[:n_prompt].reshape(x_prompt.shape)
    y_sample = x[n_prompt:].reshape(x_sample.shape)
    return (y_prompt, y_sample)
```

```python
import functools

import jax
import jax.numpy as jnp
from jax import lax
from jax.experimental import pallas as pl
from jax.experimental.pallas import tpu as pltpu

D_MODEL = 1024
GRID_W = 64
D_CONV = 512
CONV_W = 31
N_HEADS = 8
HEAD_DIM = 64
D_ATTN = N_HEADS * HEAD_DIM
MAX_WIN_H = 8
WIN_W = 16
D_FF = 2816
N_EXPERTS = 8
EPS = 1e-6

LANES = 128
SUBLANES = 8
TM = 512
TILE_ROWS = TM // GRID_W
KV_HALO = 256
CONV_HALO = 16
CONV_PAD = CONV_W // 2
N_RO = 2 * MAX_WIN_H - 1
N_CO = 2 * WIN_W - 1
HEAD_GROUP = 4
ROUTE_W = 8
ATT_UNROLL = 8
NEG_BIG = -1e30
FF_CHUNK = 256
VMEM_LIMIT = 56 * 1024 * 1024

F32 = jnp.float32
BF16 = jnp.bfloat16


def _sigmoid(x):
    return 1.0 / (1.0 + jnp.exp(-x))


def _bias_table_kernel(rpb_ref, out_ref):
    ro = pl.program_id(0)
    shape = (GRID_W, HEAD_GROUP * GRID_W)
    kcol = lax.broadcasted_iota(jnp.int32, shape, 0)
    lane = lax.broadcasted_iota(jnp.int32, shape, 1)
    head_local = lane // GRID_W
    qcol = lane - head_local * GRID_W
    start_c = jnp.clip(qcol - WIN_W // 2, 0, GRID_W - WIN_W)
    valid = (kcol >= start_c) & (kcol < start_c + WIN_W)
    co = jnp.clip(kcol - qcol + (WIN_W - 1), 0, 2 * WIN_W - 2)
    for g in range(N_HEADS // HEAD_GROUP):
        acc = jnp.zeros(shape, F32)
        for d in range(N_CO):
            val = jnp.zeros(shape, F32)
            for hl in range(HEAD_GROUP):
                h = g * HEAD_GROUP + hl
                val = jnp.where(head_local == hl, rpb_ref[(h * N_RO + ro) * N_CO + d], val)
            acc = jnp.where(co == d, val, acc)
        out_ref[0, g] = jnp.where(valid, acc, NEG_BIG)


def _bias_table(rpb_l):
    n_groups = N_HEADS // HEAD_GROUP
    width = HEAD_GROUP * GRID_W
    return pl.pallas_call(
        _bias_table_kernel,
        grid=(N_RO,),
        in_specs=[pl.BlockSpec(memory_space=pltpu.SMEM)],
        out_specs=pl.BlockSpec((1, n_groups, GRID_W, width), lambda r: (r, 0, 0, 0)),
        out_shape=jax.ShapeDtypeStruct((N_RO, n_groups, GRID_W, width), F32),
        name="bias_table",
    )(rpb_l.reshape(-1))


def _in_proj_kernel(x_ref, g_ref, w_ref, bg_ref, qg_ref, kg_ref, hm_ref,
                    vglu_ref, q_ref, k_ref, v_ref, ga_ref, gb_ref):
    x = x_ref[...]
    ms = jnp.mean(x * x, axis=-1, keepdims=True)
    h = (x * lax.rsqrt(ms + EPS) * g_ref[...]).astype(BF16)

    def proj(lo, hi):
        return jnp.dot(h, w_ref[:, lo:hi], preferred_element_type=F32)

    def head_norm(t, gain):
        msq = jnp.dot((t * t).astype(BF16), hm_ref[...], preferred_element_type=F32)
        return t * lax.rsqrt(msq + EPS) * gain

    a = proj(0, D_CONV)
    gate = proj(D_CONV, 2 * D_CONV)
    vglu_ref[...] = (a * _sigmoid(gate)).astype(BF16)
    o = 2 * D_CONV
    q_ref[...] = (head_norm(proj(o, o + D_ATTN), qg_ref[...]) * (HEAD_DIM ** -0.5)).astype(BF16)
    k_ref[...] = head_norm(proj(o + D_ATTN, o + 2 * D_ATTN), kg_ref[...]).astype(BF16)
    v_ref[...] = proj(o + 2 * D_ATTN, o + 3 * D_ATTN).astype(BF16)
    o += 3 * D_ATTN
    bg = bg_ref[...]
    ga_ref[...] = _sigmoid(proj(o, o + D_MODEL) + bg[:, :D_MODEL]).astype(BF16)
    gb_ref[...] = _sigmoid(proj(o + D_MODEL, o + 2 * D_MODEL) + bg[:, D_MODEL:]).astype(BF16)


def _in_proj(x, norm_g, w_in, b_gate, q_norm_g, k_norm_g, head_mean):
    n_tok = x.shape[0]
    d_in = w_in.shape[1]
    full = lambda shape: pl.BlockSpec(shape, lambda i: (0, 0))
    tile = lambda width: pl.BlockSpec((TM, width), lambda i: (i, 0))
    out = lambda width: jax.ShapeDtypeStruct((n_tok, width), BF16)
    return pl.pallas_call(
        _in_proj_kernel,
        grid=(n_tok // TM,),
        in_specs=[tile(D_MODEL), full((1, D_MODEL)), full((D_MODEL, d_in)),
                  full((1, 2 * D_MODEL)), full((1, D_ATTN)), full((1, D_ATTN)),
                  full((D_ATTN, D_ATTN))],
        out_specs=[tile(D_CONV), tile(D_ATTN), tile(D_ATTN), tile(D_ATTN),
                   tile(D_MODEL), tile(D_MODEL)],
        out_shape=[out(D_CONV), out(D_ATTN), out(D_ATTN), out(D_ATTN),
                   out(D_MODEL), out(D_MODEL)],
        compiler_params=pltpu.CompilerParams(
            dimension_semantics=("arbitrary",), vmem_limit_bytes=VMEM_LIMIT),
        name="in_proj",
    )(x, norm_g, w_in, b_gate, q_norm_g, k_norm_g, head_mean)


def _mixer_kernel(first_ref, last_ref,
                  x_ref, vg_ref, vgp_ref, vgn_ref, q_ref,
                  k_ref, kp_ref, kn_ref, v_ref, vp_ref, vn_ref,
                  ga_ref, gb_ref, cw_ref, cb_ref, lg_ref, lb_ref, tab_ref,
                  wa_ref, wb_ref, wo_ref, n2_ref, rw_ref, rb_ref, *rest,
                  with_router):
    if with_router:
        xo_ref, h2_ref, route_ref, route_t_ref, tot_ref = rest[:5]
    else:
        xo_ref, h2_ref = rest[:2]
    ext_ref, sh_ref, conv_ref, kext_ref, vext_ref, att_ref = rest[-6:]
    i = pl.program_id(0)
    is_first = first_ref[i]
    is_last = last_ref[i]

    keep_prev = (1 - is_first).astype(F32)
    keep_next = (1 - is_last).astype(F32)
    ext_ref[0:CONV_HALO, :] = vgp_ref[...].astype(F32) * keep_prev
    ext_ref[CONV_HALO:CONV_HALO + TM, :] = vg_ref[...].astype(F32)
    ext_ref[CONV_HALO + TM:, :] = vgn_ref[...].astype(F32) * keep_next
    n_sh_rows = sh_ref.shape[1]
    row_chunk = 128
    for cb in range(D_CONV // LANES):
        cols = slice(cb * LANES, (cb + 1) * LANES)
        for b in range(8):
            sh_ref[b] = ext_ref[b:b + n_sh_rows, cols]
        for rc in range(TM // row_chunk):
            acc = jnp.zeros((row_chunk, LANES), F32)
            for j in range(CONV_W):
                off = CONV_HALO - CONV_PAD + j
                a8, b = off // 8, off % 8
                start = rc * row_chunk + 8 * a8
                acc = acc + sh_ref[b, start:start + row_chunk, :] * cw_ref[j:j + 1, cols]
            conv_ref[rc * row_chunk:(rc + 1) * row_chunk, cols] = acc
    y = conv_ref[...] + cb_ref[...]
    mu = jnp.mean(y, axis=-1, keepdims=True)
    yc = y - mu
    var = jnp.mean(yc * yc, axis=-1, keepdims=True)
    yn = yc * lax.rsqrt(var + EPS) * lg_ref[...] + lb_ref[...]
    act = (yn * _sigmoid(yn)).astype(BF16)
    br_a = jnp.dot(act, wa_ref[...], preferred_element_type=F32)

    kext_ref[0:KV_HALO, :] = kp_ref[...]
    kext_ref[KV_HALO:KV_HALO + TM, :] = k_ref[...]
    kext_ref[KV_HALO + TM:, :] = kn_ref[...]
    vext_ref[0:KV_HALO, :] = vp_ref[...]
    vext_ref[KV_HALO:KV_HALO + TM, :] = v_ref[...]
    vext_ref[KV_HALO + TM:, :] = vn_ref[...]
    gw = HEAD_GROUP * HEAD_DIM
    lane_head = lax.broadcasted_iota(jnp.int32, (GRID_W, gw), 1) // HEAD_DIM
    win = MAX_WIN_H * GRID_W

    def row_body(j, carry):
        ws = j - MAX_WIN_H // 2
        ws = jnp.where(is_first == 1, jnp.maximum(ws, 0), ws)
        ws = jnp.where(is_last == 1, jnp.minimum(ws, 0), ws)
        ro0 = ws - j + (MAX_WIN_H - 1)
        kstart = pl.multiple_of((ws + KV_HALO // GRID_W) * GRID_W, GRID_W)
        qstart = pl.multiple_of(j * GRID_W, GRID_W)
        for g in range(N_HEADS // HEAD_GROUP):
            cols = slice(g * gw, (g + 1) * gw)
            q4 = q_ref[pl.ds(qstart, GRID_W), cols]
            qexp = jnp.concatenate(
                [jnp.where(lane_head == hl, q4, jnp.zeros_like(q4)) for hl in range(HEAD_GROUP)],
                axis=0)
            kslab = kext_ref[pl.ds(kstart, win), cols]
            st = lax.dot_general(kslab, qexp, (((1,), (1,)), ((), ())),
                                 preferred_element_type=F32)
            st = st + jnp.concatenate([tab_ref[ro0 + w, g] for w in range(MAX_WIN_H)], axis=0)
            m = jnp.max(st, axis=0, keepdims=True)
            pexp = jnp.exp(st - m)
            inv = 1.0 / jnp.sum(pexp, axis=0, keepdims=True)
            pn = (pexp * inv).astype(BF16)
            vslab = vext_ref[pl.ds(kstart, win), cols]
            o4 = lax.dot_general(pn, vslab, (((0,), (0,)), ((), ())),
                                 preferred_element_type=F32)
            out = o4[0:GRID_W]
            for hl in range(1, HEAD_GROUP):
                out = jnp.where(lane_head == hl, o4[hl * GRID_W:(hl + 1) * GRID_W], out)
            att_ref[pl.ds(qstart, GRID_W), cols] = out
        return carry

    lax.fori_loop(0, TILE_ROWS, row_body, 0, unroll=ATT_UNROLL)
    br_b = jnp.dot(att_ref[...].astype(BF16), wb_ref[...], preferred_element_type=F32)

    merged = ga_ref[...].astype(F32) * br_a + gb_ref[...].astype(F32) * br_b
    x_new = x_ref[...] + jnp.dot(merged.astype(BF16), wo_ref[...], preferred_element_type=F32)
    xo_ref[...] = x_new
    ms = jnp.mean(x_new * x_new, axis=-1, keepdims=True)
    h2 = x_new * lax.rsqrt(ms + EPS) * n2_ref[...]
    h2_ref[...] = h2.astype(h2_ref.dtype)

    if with_router:
        logits = jnp.dot(h2.astype(BF16), rw_ref[...], preferred_element_type=F32) + rb_ref[...]
        elane = lax.broadcasted_iota(jnp.int32, logits.shape, 1).astype(F32)
        logits = jnp.where(elane < N_EXPERTS, logits, -jnp.inf)
        m1 = jnp.max(logits, axis=1, keepdims=True)
        i1 = jnp.min(jnp.where(logits == m1, elane, float(LANES)), axis=1, keepdims=True)
        others = jnp.where(elane == i1, -jnp.inf, logits)
        m2 = jnp.max(others, axis=1, keepdims=True)
        i2 = jnp.min(jnp.where(others == m2, elane, float(LANES)), axis=1, keepdims=True)
        e2 = jnp.exp(m2 - m1)
        w1 = 1.0 / (1.0 + e2)
        w2 = e2 / (1.0 + e2)
        route = jnp.where(elane == 0, i1, jnp.where(elane == 1, i2, jnp.where(
            elane == 2, w1, jnp.where(elane == 3, w2, 0.0))))
        route_ref[...] = route[:, :ROUTE_W]
        route_t_ref[0] = route.T[:ROUTE_W, :]
        onehot = ((elane == i1) | (elane == i2)).astype(F32)

        @pl.when(i == 0)
        def _():
            tot_ref[...] = jnp.zeros(tot_ref.shape, F32)

        tot_ref[...] += jnp.broadcast_to(jnp.sum(onehot, axis=0, keepdims=True), tot_ref.shape)


def _mixer(x, vglu, q, k, v, ga, gb, first, last, conv_w, conv_b, ln_g, ln_b, table,
           w_a, w_b, w_o, norm2_g, router_w, router_b, *, with_router):
    n_tok = x.shape[0]
    conv_blocks = TM // CONV_HALO
    kv_blocks = TM // KV_HALO
    full2 = lambda shape: pl.BlockSpec(shape, lambda i, f, l: (0, 0))
    tile = lambda width: pl.BlockSpec((TM, width), lambda i, f, l: (i, 0))
    in_specs = [
        tile(D_MODEL),
        tile(D_CONV),
        pl.BlockSpec((CONV_HALO, D_CONV), lambda i, f, l: (i * conv_blocks - 1 + f[i], 0)),
        pl.BlockSpec((CONV_HALO, D_CONV), lambda i, f, l: ((i + 1) * conv_blocks - l[i], 0)),
        tile(D_ATTN),
        tile(D_ATTN),
        pl.BlockSpec((KV_HALO, D_ATTN), lambda i, f, l: (i * kv_blocks - 1 + f[i], 0)),
        pl.BlockSpec((KV_HALO, D_ATTN), lambda i, f, l: ((i + 1) * kv_blocks - l[i], 0)),
        tile(D_ATTN),
        pl.BlockSpec((KV_HALO, D_ATTN), lambda i, f, l: (i * kv_blocks - 1 + f[i], 0)),
        pl.BlockSpec((KV_HALO, D_ATTN), lambda i, f, l: ((i + 1) * kv_blocks - l[i], 0)),
        tile(D_MODEL),
        tile(D_MODEL),
        full2((CONV_W + 1, D_CONV)),
        full2((1, D_CONV)), full2((1, D_CONV)), full2((1, D_CONV)),
        pl.BlockSpec(table.shape, lambda i, f, l: (0, 0, 0, 0)),
        full2((D_CONV, D_MODEL)), full2((D_ATTN, D_MODEL)), full2((D_MODEL, D_MODEL)),
        full2((1, D_MODEL)),
        full2((D_MODEL, LANES)), full2((1, LANES)),
    ]
    out_specs = [tile(D_MODEL), tile(D_MODEL)]
    out_shape = [jax.ShapeDtypeStruct((n_tok, D_MODEL), F32),
                 jax.ShapeDtypeStruct((n_tok, D_MODEL), F32 if with_router else BF16)]
    if with_router:
        out_specs += [tile(ROUTE_W),
                      pl.BlockSpec((1, ROUTE_W, TM), lambda i, f, l: (i, 0, 0)),
                      pl.BlockSpec((8, LANES), lambda i, f, l: (0, 0))]
        out_shape += [jax.ShapeDtypeStruct((n_tok, ROUTE_W), F32),
                      jax.ShapeDtypeStruct((n_tok // TM, ROUTE_W, TM), F32),
                      jax.ShapeDtypeStruct((8, LANES), F32)]
    ext_rows = TM + 2 * CONV_HALO
    scratch = [
        pltpu.VMEM((ext_rows, D_CONV), F32),
        pltpu.VMEM((8, ext_rows - 8, LANES), F32),
        pltpu.VMEM((TM, D_CONV), F32),
        pltpu.VMEM((TM + 2 * KV_HALO, D_ATTN), BF16),
        pltpu.VMEM((TM + 2 * KV_HALO, D_ATTN), BF16),
        pltpu.VMEM((TM, D_ATTN), F32),
    ]
    return pl.pallas_call(
        functools.partial(_mixer_kernel, with_router=with_router),
        grid_spec=pltpu.PrefetchScalarGridSpec(
            num_scalar_prefetch=2, grid=(n_tok // TM,),
            in_specs=in_specs, out_specs=out_specs, scratch_shapes=scratch),
        out_shape=out_shape,
        compiler_params=pltpu.CompilerParams(
            dimension_semantics=("arbitrary",), vmem_limit_bytes=VMEM_LIMIT),
        name="mixer",
    )(first, last, x, vglu, vglu, vglu, q, k, k, k, v, v, v, ga, gb,
      conv_w, conv_b, ln_g, ln_b, table, w_a, w_b, w_o, norm2_g, router_w, router_b)


def _ffn_kernel(h_ref, x_ref, w1_ref, w3_ref, w2_ref, o_ref):
    h = h_ref[...]
    acc = x_ref[...]
    for c in range(0, D_FF, FF_CHUNK):
        a = jnp.dot(h, w1_ref[:, c:c + FF_CHUNK], preferred_element_type=F32)
        b = jnp.dot(h, w3_ref[:, c:c + FF_CHUNK], preferred_element_type=F32)
        act = (a * _sigmoid(a) * b).astype(BF16)
        acc = acc + jnp.dot(act, w2_ref[c:c + FF_CHUNK, :], preferred_element_type=F32)
    o_ref[...] = acc


def _ffn(h2, x, w1, w3, w2):
    n_tok = x.shape[0]
    tile = lambda: pl.BlockSpec((TM, D_MODEL), lambda i: (i, 0))
    single = dict(pipeline_mode=pl.Buffered(1))
    return pl.pallas_call(
        _ffn_kernel,
        grid=(n_tok // TM,),
        in_specs=[tile(), tile(),
                  pl.BlockSpec((D_MODEL, D_FF), lambda i: (0, 0), **single),
                  pl.BlockSpec((D_MODEL, D_FF), lambda i: (0, 0), **single),
                  pl.BlockSpec((D_FF, D_MODEL), lambda i: (0, 0), **single)],
        out_specs=tile(),
        out_shape=jax.ShapeDtypeStruct((n_tok, D_MODEL), F32),
        compiler_params=pltpu.CompilerParams(
            dimension_semantics=("arbitrary",), vmem_limit_bytes=VMEM_LIMIT),
        name="ffn",
    )(h2, x, w1, w3, w2)


def _slots_kernel(rt_ref, start_ref, slots_ref, run_ref):
    i = pl.program_id(0)

    @pl.when(i == 0)
    def _():
        run_ref[...] = jnp.zeros(run_ref.shape, F32)

    rt = rt_ref[0]
    eidx = lax.broadcasted_iota(jnp.int32, (N_EXPERTS, TM), 0).astype(F32)
    e1 = eidx == rt[0:1, :]
    e2 = eidx == rt[1:2, :]
    chosen = (e1 | e2).astype(F32)
    before = lax.broadcasted_iota(jnp.int32, (TM, TM), 0) < lax.broadcasted_iota(
        jnp.int32, (TM, TM), 1)
    rank = jnp.dot(chosen.astype(BF16), before.astype(BF16), preferred_element_type=F32)
    pos = rank + start_ref[:, 0:1] + run_ref[:, 0:1]
    s1 = jnp.sum(jnp.where(e1, pos, 0.0), axis=0, keepdims=True)
    s2 = jnp.sum(jnp.where(e2, pos, 0.0), axis=0, keepdims=True)
    slots_ref[0] = jnp.concatenate([s1, s2], axis=0).astype(jnp.int32)
    run_ref[...] += jnp.broadcast_to(jnp.sum(chosen, axis=1, keepdims=True), run_ref.shape)


def _slots(route_t, start):
    n_tiles = route_t.shape[0]
    return pl.pallas_call(
        _slots_kernel,
        grid=(n_tiles,),
        in_specs=[pl.BlockSpec((1, ROUTE_W, TM), lambda i: (i, 0, 0)),
                  pl.BlockSpec((N_EXPERTS, LANES), lambda i: (0, 0))],
        out_specs=pl.BlockSpec((1, 2, TM), lambda i: (i, 0, 0)),
        out_shape=jax.ShapeDtypeStruct((n_tiles, 2, TM), jnp.int32),
        scratch_shapes=[pltpu.VMEM((N_EXPERTS, LANES), F32)],
        compiler_params=pltpu.CompilerParams(dimension_semantics=("arbitrary",)),
        name="moe_slots",
    )(route_t, start)


ROW_DMA_UNROLL = 8


ZERO_ROWS = 64


def _dispatch_kernel(ts_ref, tl_ref, slots_ref, h_ref, xs_ref, zero_ref, sem, zsem):
    i = pl.program_id(0)

    def fill_padding(wait):
        def run(copy):
            copy.wait() if wait else copy.start()

        for e in range(N_EXPERTS):
            first = ts_ref[e]
            head = (-first) & (SUBLANES - 1)
            for r in range(SUBLANES - 1):
                @pl.when(r < head)
                def _():
                    run(pltpu.make_async_copy(
                        zero_ref.at[pl.ds(0, 1)], xs_ref.at[pl.ds(first + r, 1)], zsem))

            base = first + head
            n = tl_ref[e] - head
            full = n // ZERO_ROWS

            def chunk(c, carry):
                off = pl.multiple_of(base + c * ZERO_ROWS, SUBLANES)
                run(pltpu.make_async_copy(zero_ref, xs_ref.at[pl.ds(off, ZERO_ROWS)], zsem))
                return carry

            lax.fori_loop(0, full, chunk, 0)
            rem_base = base + full * ZERO_ROWS
            for b in range(SUBLANES.bit_length() - 1, ZERO_ROWS.bit_length() - 1):
                size = 1 << b

                @pl.when(((n >> b) & 1) == 1)
                def _():
                    off = pl.multiple_of(rem_base + (n & (size - 1)), SUBLANES)
                    run(pltpu.make_async_copy(
                        zero_ref.at[pl.ds(0, size)], xs_ref.at[pl.ds(off, size)], zsem))

    @pl.when(i == 0)
    def _():
        zero_ref[...] = jnp.zeros(zero_ref.shape, F32)
        fill_padding(wait=False)

    def issue(t, carry):
        for c in range(2):
            dst = slots_ref[c * TM + t]
            pltpu.make_async_copy(h_ref.at[pl.ds(t, 1)], xs_ref.at[pl.ds(dst, 1)], sem).start()
        return carry

    lax.fori_loop(0, TM, issue, 0, unroll=ROW_DMA_UNROLL)
    for c in range(2):
        pltpu.make_async_copy(h_ref, xs_ref.at[pl.ds(0, TM)], sem).wait()

    @pl.when(i == 0)
    def _():
        fill_padding(wait=True)


def _dispatch(tail_start, tail_len, slots_flat, h2, n_rows):
    n_tok = h2.shape[0]
    return pl.pallas_call(
        _dispatch_kernel,
        grid_spec=pltpu.PrefetchScalarGridSpec(
            num_scalar_prefetch=2, grid=(n_tok // TM,),
            in_specs=[pl.BlockSpec((2 * TM,), lambda i, ts, tl: (i,), memory_space=pltpu.SMEM),
                      pl.BlockSpec((TM, D_MODEL), lambda i, ts, tl: (i, 0))],
            out_specs=pl.BlockSpec(memory_space=pl.ANY),
            scratch_shapes=[pltpu.VMEM((ZERO_ROWS, D_MODEL), F32),
                            pltpu.SemaphoreType.DMA, pltpu.SemaphoreType.DMA]),
        out_shape=jax.ShapeDtypeStruct((n_rows, D_MODEL), F32),
        compiler_params=pltpu.CompilerParams(dimension_semantics=("arbitrary",)),
        name="moe_dispatch",
    )(tail_start, tail_len, slots_flat, h2)


def _expert_kernel(te_ref, na_ref, x_ref, w1_ref, w3_ref, w2_ref, y_ref):
    j = pl.program_id(0)

    @pl.when(j < na_ref[0])
    def _():
        h = x_ref[...].astype(BF16)
        acc = jnp.zeros((TM, D_MODEL), F32)
        for c in range(0, D_FF, FF_CHUNK):
            a = jnp.dot(h, w1_ref[0, :, c:c + FF_CHUNK], preferred_element_type=F32)
            b = jnp.dot(h, w3_ref[0, :, c:c + FF_CHUNK], preferred_element_type=F32)
            act = (a * _sigmoid(a) * b).astype(BF16)
            acc = acc + jnp.dot(act, w2_ref[0, c:c + FF_CHUNK, :], preferred_element_type=F32)
        y_ref[...] = acc

    @pl.when(j >= na_ref[0])
    def _():
        y_ref[...] = jnp.zeros(y_ref.shape, F32)


def _experts(tile_expert, n_active, xs, w1, w3, w2):
    n_tiles = xs.shape[0] // TM
    weights = lambda shape: pl.BlockSpec(shape, lambda j, te, na: (te[j], 0, 0))
    return pl.pallas_call(
        _expert_kernel,
        grid_spec=pltpu.PrefetchScalarGridSpec(
            num_scalar_prefetch=2, grid=(n_tiles,),
            in_specs=[pl.BlockSpec((TM, D_MODEL),
                                   lambda j, te, na: (jnp.minimum(j, na[0] - 1), 0)),
                      weights((1, D_MODEL, D_FF)), weights((1, D_MODEL, D_FF)),
                      weights((1, D_FF, D_MODEL))],
            out_specs=pl.BlockSpec((TM, D_MODEL), lambda j, te, na: (j, 0))),
        out_shape=jax.ShapeDtypeStruct(xs.shape, F32),
        compiler_params=pltpu.CompilerParams(
            dimension_semantics=("arbitrary",), vmem_limit_bytes=VMEM_LIMIT),
        name="moe_experts",
    )(tile_expert, n_active, xs, w1, w3, w2)


def _combine_kernel(slots_ref, x_ref, route_ref, y_ref, o_ref, stage_ref, sem):
    def issue(t, carry):
        for c in range(2):
            src = slots_ref[c * TM + t]
            pltpu.make_async_copy(y_ref.at[pl.ds(src, 1)],
                                  stage_ref.at[pl.ds(c * TM + t, 1)], sem).start()
        return carry

    lax.fori_loop(0, TM, issue, 0, unroll=ROW_DMA_UNROLL)
    for c in range(2):
        pltpu.make_async_copy(y_ref.at[pl.ds(0, TM)], stage_ref.at[pl.ds(c * TM, TM)], sem).wait()
    route = route_ref[...]
    o_ref[...] = (x_ref[...] + route[:, 2:3] * stage_ref[0:TM, :]
                  + route[:, 3:4] * stage_ref[TM:2 * TM, :])


def _combine(slots_flat, x, route, y):
    n_tok = x.shape[0]
    return pl.pallas_call(
        _combine_kernel,
        grid=(n_tok // TM,),
        in_specs=[pl.BlockSpec((2 * TM,), lambda i: (i,), memory_space=pltpu.SMEM),
                  pl.BlockSpec((TM, D_MODEL), lambda i: (i, 0)),
                  pl.BlockSpec((TM, ROUTE_W), lambda i: (i, 0)),
                  pl.BlockSpec(memory_space=pl.ANY)],
        out_specs=pl.BlockSpec((TM, D_MODEL), lambda i: (i, 0)),
        out_shape=jax.ShapeDtypeStruct((n_tok, D_MODEL), F32),
        scratch_shapes=[pltpu.VMEM((2 * TM, D_MODEL), F32), pltpu.SemaphoreType.DMA],
        compiler_params=pltpu.CompilerParams(
            dimension_semantics=("arbitrary",), vmem_limit_bytes=VMEM_LIMIT),
        name="moe_combine",
    )(slots_flat, x, route, y)


def _moe(h2, x, route, route_t, totals, w1, w3, w2):
    n_tok = x.shape[0]
    tot = totals[0, :N_EXPERTS].astype(jnp.int32)
    padded = (tot + TM - 1) // TM * TM
    end = jnp.cumsum(padded)
    start = end - padded
    n_tiles = (2 * n_tok + N_EXPERTS * (TM - 1)) // TM
    n_active = end[-1] // TM
    tile = jnp.minimum(jnp.arange(n_tiles, dtype=jnp.int32), n_active - 1)
    tile_expert = jnp.sum(tile[:, None] * TM >= end[None, :], axis=1).astype(jnp.int32)
    start_b = jnp.broadcast_to(start.astype(F32)[:, None], (N_EXPERTS, LANES))
    tail_start = (start + tot).astype(jnp.int32)
    tail_len = (end.at[-1].set(n_tiles * TM) - tail_start).astype(jnp.int32)

    slots_flat = _slots(route_t, start_b).reshape(-1)
    xs = _dispatch(tail_start, tail_len, slots_flat, h2, n_tiles * TM)
    y = _experts(tile_expert, n_active.reshape(1).astype(jnp.int32), xs, w1, w3, w2)
    return _combine(slots_flat, x, route, y)


def _tile_edges(seq_lens):
    first, last = [], []
    for n in seq_lens:
        t = n // TM
        first += [1] + [0] * (t - 1)
        last += [0] * (t - 1) + [1]
    return jnp.asarray(first, jnp.int32), jnp.asarray(last, jnp.int32)


def kernel(x_prompt, x_sample, norm1_g, w_in, b_gate, conv_w, conv_b, conv_ln_g, conv_ln_b,
           q_norm_g, k_norm_g, rpb, w_proj_a, w_proj_b, w_out, norm2_g, ffn_w1, ffn_w3, ffn_w2,
           router_w, router_b, moe_w1, moe_w3, moe_w2):
    depth = w_in.shape[0]
    seq_lens = [x_prompt.shape[1]] * x_prompt.shape[0] + [x_sample.shape[1]] * x_sample.shape[0]
    assert all(n % TM == 0 and n // GRID_W >= 2 * TILE_ROWS for n in seq_lens)
    first, last = _tile_edges(seq_lens)
    n_prompt = x_prompt.shape[0] * x_prompt.shape[1]
    x = jnp.concatenate([x_prompt.reshape(-1, D_MODEL), x_sample.reshape(-1, D_MODEL)], axis=0)

    head_id = jnp.arange(D_ATTN) // HEAD_DIM
    head_mean = jnp.where(head_id[:, None] == head_id[None, :], 1.0 / HEAD_DIM, 0.0).astype(BF16)
    row = lambda a: a.reshape(1, -1).astype(F32)

    for l in range(depth):
        table = _bias_table(rpb[l])
        vglu, q, k, v, ga, gb = _in_proj(
            x, row(norm1_g[l]), w_in[l].astype(BF16), row(b_gate[l]),
            row(jnp.tile(q_norm_g[l], N_HEADS)), row(jnp.tile(k_norm_g[l], N_HEADS)), head_mean)
        j = l // 2
        moe = l % 2 == 1
        if moe:
            rw = jnp.pad(router_w[j], ((0, 0), (0, LANES - N_EXPERTS))).astype(BF16)
            rb = jnp.pad(row(router_b[j]), ((0, 0), (0, LANES - N_EXPERTS)))
        else:
            rw = jnp.zeros((D_MODEL, LANES), BF16)
            rb = jnp.zeros((1, LANES), F32)
        cw = jnp.pad(conv_w[l], ((0, 1), (0, 0)))
        outs = _mixer(
            x, vglu, q, k, v, ga, gb, first, last, cw, row(conv_b[l]), row(conv_ln_g[l]),
            row(conv_ln_b[l]), table, w_proj_a[l].astype(BF16), w_proj_b[l].astype(BF16),
            w_out[l].astype(BF16), row(norm2_g[l]), rw, rb, with_router=moe)
        if moe:
            x, h2, route, route_t, totals = outs
            x = _moe(h2, x, route, route_t, totals, moe_w1[j].astype(BF16),
                     moe_w3[j].astype(BF16), moe_w2[j].astype(BF16))
        else:
            x, h2 = outs
            x = _ffn(h2, x, ffn_w1[j].astype(BF16), ffn_w3[j].astype(BF16),
                     ffn_w2[j].astype(BF16))

    y_prompt = x[:n_prompt].reshape(x_prompt.shape)
    y_sample = x[n_prompt:].reshape(x_sample.shape)
    return (y_prompt, y_sample)
```

```python
import functools

import jax
import jax.numpy as jnp
from jax import lax
from jax.experimental import pallas as pl
from jax.experimental.pallas import tpu as pltpu

D_MODEL = 1024
GRID_W = 64
D_CONV = 512
CONV_W = 31
N_HEADS = 8
HEAD_DIM = 64
D_ATTN = N_HEADS * HEAD_DIM
MAX_WIN_H = 8
WIN_W = 16
D_FF = 2816
N_EXPERTS = 8
EPS = 1e-6

LANES = 128
SUBLANES = 8
TM = 512
TILE_ROWS = TM // GRID_W
KV_HALO = 256
CONV_HALO = 16
CONV_PAD = CONV_W // 2
N_RO = 2 * MAX_WIN_H - 1
N_CO = 2 * WIN_W - 1
HEAD_GROUP = 4
ROUTE_W = 8
ATT_UNROLL = 8
NEG_BIG = -1e30
LOG2E = 1.4426950408889634
FF_CHUNK = 256
VMEM_LIMIT = 56 * 1024 * 1024

F32 = jnp.float32
BF16 = jnp.bfloat16


def _sigmoid(x):
    return 1.0 / (1.0 + jnp.exp(-x))


def _part_specs(parts, width=D_MODEL):
    starts, specs, tile0 = [], [], 0
    for a in parts:
        n = a.shape[0] // TM
        specs.append(pl.BlockSpec(
            (TM, width), lambda i, *_, s=tile0, n=n: (jnp.clip(i - s, 0, n - 1), 0)))
        starts.append(tile0)
        tile0 += n
    return tuple(starts), specs, tile0


def _read_parts(refs, starts):
    i = pl.program_id(0)
    x = refs[0][...]
    for ref, s in zip(refs[1:], starts[1:]):
        x = jnp.where(i >= s, ref[...], x)
    return x


def _write_parts(refs, starts, value):
    i = pl.program_id(0)
    ends = starts[1:] + (None,)
    for ref, s, e in zip(refs, starts, ends):
        owns = i >= s if e is None else (i >= s) & (i < e)

        @pl.when(owns)
        def _():
            ref[...] = value


CAST_ROWS = 512


def _cast_kernel(w_ref, o_ref):
    o_ref[...] = w_ref[...].astype(BF16)


def _to_bf16(w):
    lead = w.shape[:-2]
    k, n = w.shape[-2:]
    w3 = w.reshape((-1, k, n))
    rows = next((r for r in (CAST_ROWS, CAST_ROWS // 2, CAST_ROWS // 4) if k % r == 0), k)
    spec = pl.BlockSpec((1, rows, n), lambda e, r: (e, r, 0))
    out = pl.pallas_call(
        _cast_kernel,
        grid=(w3.shape[0], k // rows),
        in_specs=[spec], out_specs=spec,
        out_shape=jax.ShapeDtypeStruct(w3.shape, BF16),
        compiler_params=pltpu.CompilerParams(
            dimension_semantics=("arbitrary", "arbitrary"), vmem_limit_bytes=VMEM_LIMIT),
        name="to_bf16",
    )(w3)
    return out.reshape(lead + (k, n))


def _bias_table_kernel(rpb_ref, out_ref):
    ro = pl.program_id(0)
    shape = (GRID_W, HEAD_GROUP * GRID_W)
    kcol = lax.broadcasted_iota(jnp.int32, shape, 0)
    lane = lax.broadcasted_iota(jnp.int32, shape, 1)
    head_local = lane // GRID_W
    qcol = lane - head_local * GRID_W
    start_c = jnp.clip(qcol - WIN_W // 2, 0, GRID_W - WIN_W)
    valid = (kcol >= start_c) & (kcol < start_c + WIN_W)
    co = jnp.clip(kcol - qcol + (WIN_W - 1), 0, 2 * WIN_W - 2)
    for g in range(N_HEADS // HEAD_GROUP):
        acc = jnp.zeros(shape, F32)
        for d in range(N_CO):
            val = jnp.zeros(shape, F32)
            for hl in range(HEAD_GROUP):
                h = g * HEAD_GROUP + hl
                val = jnp.where(head_local == hl, rpb_ref[(h * N_RO + ro) * N_CO + d], val)
            acc = jnp.where(co == d, val, acc)
        out_ref[0, g] = jnp.where(valid, acc * LOG2E, NEG_BIG)


def _bias_table(rpb_l):
    n_groups = N_HEADS // HEAD_GROUP
    width = HEAD_GROUP * GRID_W
    return pl.pallas_call(
        _bias_table_kernel,
        grid=(N_RO,),
        in_specs=[pl.BlockSpec(memory_space=pltpu.SMEM)],
        out_specs=pl.BlockSpec((1, n_groups, GRID_W, width), lambda r: (r, 0, 0, 0)),
        out_shape=jax.ShapeDtypeStruct((N_RO, n_groups, GRID_W, width), F32),
        name="bias_table",
    )(rpb_l.reshape(-1))


def _in_proj_kernel(*refs, x_starts):
    n_parts = len(x_starts)
    (g_ref, w_ref, bg_ref, qg_ref, kg_ref, hm_ref,
     vglu_ref, q_ref, k_ref, v_ref, ga_ref, gb_ref) = refs[n_parts:]
    x = _read_parts(refs[:n_parts], x_starts)
    ms = jnp.mean(x * x, axis=-1, keepdims=True)
    h = (x * lax.rsqrt(ms + EPS) * g_ref[...]).astype(BF16)

    def proj(lo, hi):
        return jnp.dot(h, w_ref[:, lo:hi], preferred_element_type=F32)

    def head_norm(t, gain):
        msq = jnp.dot((t * t).astype(BF16), hm_ref[...], preferred_element_type=F32)
        return t * lax.rsqrt(msq + EPS) * gain

    a = proj(0, D_CONV)
    gate = proj(D_CONV, 2 * D_CONV)
    vglu_ref[...] = (a * _sigmoid(gate)).astype(BF16)
    o = 2 * D_CONV
    q_ref[...] = (head_norm(proj(o, o + D_ATTN), qg_ref[...])
                  * (HEAD_DIM ** -0.5 * LOG2E)).astype(BF16)
    k_ref[...] = head_norm(proj(o + D_ATTN, o + 2 * D_ATTN), kg_ref[...]).astype(BF16)
    v_ref[...] = proj(o + 2 * D_ATTN, o + 3 * D_ATTN).astype(BF16)
    o += 3 * D_ATTN
    bg = bg_ref[...]
    ga_ref[...] = _sigmoid(proj(o, o + D_MODEL) + bg[:, :D_MODEL]).astype(BF16)
    gb_ref[...] = _sigmoid(proj(o + D_MODEL, o + 2 * D_MODEL) + bg[:, D_MODEL:]).astype(BF16)


def _in_proj(x_parts, norm_g, w_in, b_gate, q_norm_g, k_norm_g, head_mean):
    x_starts, x_specs, n_tiles = _part_specs(x_parts)
    n_tok = n_tiles * TM
    d_in = w_in.shape[1]
    full = lambda shape: pl.BlockSpec(shape, lambda i: (0, 0))
    tile = lambda width: pl.BlockSpec((TM, width), lambda i: (i, 0))
    out = lambda width: jax.ShapeDtypeStruct((n_tok, width), BF16)
    return pl.pallas_call(
        functools.partial(_in_proj_kernel, x_starts=x_starts),
        grid=(n_tiles,),
        in_specs=x_specs + [full((1, D_MODEL)), full((D_MODEL, d_in)),
                            full((1, 2 * D_MODEL)), full((1, D_ATTN)), full((1, D_ATTN)),
                            full((D_ATTN, D_ATTN))],
        out_specs=[tile(D_CONV), tile(D_ATTN), tile(D_ATTN), tile(D_ATTN),
                   tile(D_MODEL), tile(D_MODEL)],
        out_shape=[out(D_CONV), out(D_ATTN), out(D_ATTN), out(D_ATTN),
                   out(D_MODEL), out(D_MODEL)],
        compiler_params=pltpu.CompilerParams(
            dimension_semantics=("arbitrary",), vmem_limit_bytes=VMEM_LIMIT),
        name="in_proj",
    )(*x_parts, norm_g, w_in, b_gate, q_norm_g, k_norm_g, head_mean)


def _mixer_kernel(first_ref, last_ref, *refs, x_starts, with_router):
    n_parts = len(x_starts)
    (vg_ref, vgp_ref, vgn_ref, q_ref,
     k_ref, kp_ref, kn_ref, v_ref, vp_ref, vn_ref,
     ga_ref, gb_ref, cw_ref, cb_ref, lg_ref, lb_ref, tab_ref,
     wa_ref, wb_ref, wo_ref, n2_ref, rw_ref, rb_ref) = refs[n_parts:n_parts + 23]
    rest = refs[n_parts + 23:]
    if with_router:
        xo_ref, h2_ref, route_ref, route_t_ref, tot_ref = rest[:5]
    else:
        xo_ref, h2_ref = rest[:2]
    ext_ref, sh_ref, conv_ref, kext_ref, vext_ref, att_ref = rest[-6:]
    i = pl.program_id(0)
    is_first = first_ref[i]
    is_last = last_ref[i]

    keep_prev = (1 - is_first).astype(F32)
    keep_next = (1 - is_last).astype(F32)
    ext_ref[0:CONV_HALO, :] = vgp_ref[...].astype(F32) * keep_prev
    ext_ref[CONV_HALO:CONV_HALO + TM, :] = vg_ref[...].astype(F32)
    ext_ref[CONV_HALO + TM:, :] = vgn_ref[...].astype(F32) * keep_next
    n_sh_rows = sh_ref.shape[1]
    row_chunk = 128
    for cb in range(D_CONV // LANES):
        cols = slice(cb * LANES, (cb + 1) * LANES)
        for b in range(8):
            sh_ref[b] = ext_ref[b:b + n_sh_rows, cols]
        def conv_rows(rc, carry, cols=cols):
            base = pl.multiple_of(rc * row_chunk, row_chunk)
            acc = jnp.zeros((row_chunk, LANES), F32)
            for j in range(CONV_W):
                off = CONV_HALO - CONV_PAD + j
                a8, b = off // SUBLANES, off % SUBLANES
                acc = acc + (sh_ref[b, pl.ds(base + SUBLANES * a8, row_chunk), :]
                             * cw_ref[j:j + 1, cols])
            conv_ref[pl.ds(base, row_chunk), cols] = acc
            return carry

        lax.fori_loop(0, TM // row_chunk, conv_rows, 0)
    y = conv_ref[...] + cb_ref[...]
    mu = jnp.mean(y, axis=-1, keepdims=True)
    yc = y - mu
    var = jnp.mean(yc * yc, axis=-1, keepdims=True)
    yn = yc * lax.rsqrt(var + EPS) * lg_ref[...] + lb_ref[...]
    act = (yn * _sigmoid(yn)).astype(BF16)
    br_a = jnp.dot(act, wa_ref[...], preferred_element_type=F32)

    kext_ref[0:KV_HALO, :] = kp_ref[...]
    kext_ref[KV_HALO:KV_HALO + TM, :] = k_ref[...]
    kext_ref[KV_HALO + TM:, :] = kn_ref[...]
    vext_ref[0:KV_HALO, :] = vp_ref[...]
    vext_ref[KV_HALO:KV_HALO + TM, :] = v_ref[...]
    vext_ref[KV_HALO + TM:, :] = vn_ref[...]
    gw = HEAD_GROUP * HEAD_DIM
    lane_head = lax.broadcasted_iota(jnp.int32, (GRID_W, gw), 1) // HEAD_DIM
    win = MAX_WIN_H * GRID_W

    def row_body(j, carry):
        ws = j - MAX_WIN_H // 2
        ws = jnp.where(is_first == 1, jnp.maximum(ws, 0), ws)
        ws = jnp.where(is_last == 1, jnp.minimum(ws, 0), ws)
        ro0 = ws - j + (MAX_WIN_H - 1)
        kstart = pl.multiple_of((ws + KV_HALO // GRID_W) * GRID_W, GRID_W)
        qstart = pl.multiple_of(j * GRID_W, GRID_W)
        for g in range(N_HEADS // HEAD_GROUP):
            cols = slice(g * gw, (g + 1) * gw)
            q4 = q_ref[pl.ds(qstart, GRID_W), cols]
            qexp = jnp.concatenate(
                [jnp.where(lane_head == hl, q4, jnp.zeros_like(q4)) for hl in range(HEAD_GROUP)],
                axis=0)
            kslab = kext_ref[pl.ds(kstart, win), cols]
            st = lax.dot_general(kslab, qexp, (((1,), (1,)), ((), ())),
                                 preferred_element_type=F32)
            st = st + jnp.concatenate([tab_ref[ro0 + w, g] for w in range(MAX_WIN_H)], axis=0)
            m = jnp.max(st, axis=0, keepdims=True)
            pexp = jnp.exp2(st - m)
            inv = 1.0 / jnp.sum(pexp, axis=0, keepdims=True)
            pn = (pexp * inv).astype(BF16)
            vslab = vext_ref[pl.ds(kstart, win), cols]
            o4 = lax.dot_general(pn, vslab, (((0,), (0,)), ((), ())),
                                 preferred_element_type=F32)
            out = o4[0:GRID_W]
            for hl in range(1, HEAD_GROUP):
                out = jnp.where(lane_head == hl, o4[hl * GRID_W:(hl + 1) * GRID_W], out)
            att_ref[pl.ds(qstart, GRID_W), cols] = out
        return carry

    lax.fori_loop(0, TILE_ROWS, row_body, 0, unroll=ATT_UNROLL)
    br_b = jnp.dot(att_ref[...].astype(BF16), wb_ref[...], preferred_element_type=F32)

    merged = ga_ref[...].astype(F32) * br_a + gb_ref[...].astype(F32) * br_b
    x_new = (_read_parts(refs[:n_parts], x_starts)
             + jnp.dot(merged.astype(BF16), wo_ref[...], preferred_element_type=F32))
    xo_ref[...] = x_new
    ms = jnp.mean(x_new * x_new, axis=-1, keepdims=True)
    h2 = x_new * lax.rsqrt(ms + EPS) * n2_ref[...]
    h2_ref[...] = h2.astype(h2_ref.dtype)

    if with_router:
        logits = jnp.dot(h2.astype(BF16), rw_ref[...], preferred_element_type=F32) + rb_ref[...]
        elane = lax.broadcasted_iota(jnp.int32, logits.shape, 1).astype(F32)
        logits = jnp.where(elane < N_EXPERTS, logits, -jnp.inf)
        m1 = jnp.max(logits, axis=1, keepdims=True)
        i1 = jnp.min(jnp.where(logits == m1, elane, float(LANES)), axis=1, keepdims=True)
        others = jnp.where(elane == i1, -jnp.inf, logits)
        m2 = jnp.max(others, axis=1, keepdims=True)
        i2 = jnp.min(jnp.where(others == m2, elane, float(LANES)), axis=1, keepdims=True)
        e2 = jnp.exp(m2 - m1)
        w1 = 1.0 / (1.0 + e2)
        w2 = e2 / (1.0 + e2)
        route = jnp.where(elane == 0, i1, jnp.where(elane == 1, i2, jnp.where(
            elane == 2, w1, jnp.where(elane == 3, w2, 0.0))))
        route_ref[...] = route[:, :ROUTE_W]
        route_t_ref[0] = route.T[:ROUTE_W, :]
        onehot = ((elane == i1) | (elane == i2)).astype(F32)

        @pl.when(i == 0)
        def _():
            tot_ref[...] = jnp.zeros(tot_ref.shape, F32)

        tot_ref[...] += jnp.broadcast_to(jnp.sum(onehot, axis=0, keepdims=True), tot_ref.shape)


def _mixer(x_parts, vglu, q, k, v, ga, gb, first, last, conv_w, conv_b, ln_g, ln_b, table,
           w_a, w_b, w_o, norm2_g, router_w, router_b, *, with_router):
    x_starts, x_specs, n_tiles = _part_specs(x_parts)
    n_tok = n_tiles * TM
    conv_blocks = TM // CONV_HALO
    kv_blocks = TM // KV_HALO
    full2 = lambda shape: pl.BlockSpec(shape, lambda i, f, l: (0, 0))
    tile = lambda width: pl.BlockSpec((TM, width), lambda i, f, l: (i, 0))
    in_specs = x_specs + [
        tile(D_CONV),
        pl.BlockSpec((CONV_HALO, D_CONV), lambda i, f, l: (i * conv_blocks - 1 + f[i], 0)),
        pl.BlockSpec((CONV_HALO, D_CONV), lambda i, f, l: ((i + 1) * conv_blocks - l[i], 0)),
        tile(D_ATTN),
        tile(D_ATTN),
        pl.BlockSpec((KV_HALO, D_ATTN), lambda i, f, l: (i * kv_blocks - 1 + f[i], 0)),
        pl.BlockSpec((KV_HALO, D_ATTN), lambda i, f, l: ((i + 1) * kv_blocks - l[i], 0)),
        tile(D_ATTN),
        pl.BlockSpec((KV_HALO, D_ATTN), lambda i, f, l: (i * kv_blocks - 1 + f[i], 0)),
        pl.BlockSpec((KV_HALO, D_ATTN), lambda i, f, l: ((i + 1) * kv_blocks - l[i], 0)),
        tile(D_MODEL),
        tile(D_MODEL),
        full2((CONV_W + 1, D_CONV)),
        full2((1, D_CONV)), full2((1, D_CONV)), full2((1, D_CONV)),
        pl.BlockSpec(table.shape, lambda i, f, l: (0, 0, 0, 0)),
        full2((D_CONV, D_MODEL)), full2((D_ATTN, D_MODEL)), full2((D_MODEL, D_MODEL)),
        full2((1, D_MODEL)),
        full2((D_MODEL, LANES)), full2((1, LANES)),
    ]
    out_specs = [tile(D_MODEL), tile(D_MODEL)]
    out_shape = [jax.ShapeDtypeStruct((n_tok, D_MODEL), F32),
                 jax.ShapeDtypeStruct((n_tok, D_MODEL), F32 if with_router else BF16)]
    if with_router:
        out_specs += [tile(ROUTE_W),
                      pl.BlockSpec((1, ROUTE_W, TM), lambda i, f, l: (i, 0, 0)),
                      pl.BlockSpec((8, LANES), lambda i, f, l: (0, 0))]
        out_shape += [jax.ShapeDtypeStruct((n_tok, ROUTE_W), F32),
                      jax.ShapeDtypeStruct((n_tok // TM, ROUTE_W, TM), F32),
                      jax.ShapeDtypeStruct((8, LANES), F32)]
    ext_rows = TM + 2 * CONV_HALO
    scratch = [
        pltpu.VMEM((ext_rows, D_CONV), F32),
        pltpu.VMEM((8, ext_rows - 8, LANES), F32),
        pltpu.VMEM((TM, D_CONV), F32),
        pltpu.VMEM((TM + 2 * KV_HALO, D_ATTN), BF16),
        pltpu.VMEM((TM + 2 * KV_HALO, D_ATTN), BF16),
        pltpu.VMEM((TM, D_ATTN), F32),
    ]
    return pl.pallas_call(
        functools.partial(_mixer_kernel, x_starts=x_starts, with_router=with_router),
        grid_spec=pltpu.PrefetchScalarGridSpec(
            num_scalar_prefetch=2, grid=(n_tiles,),
            in_specs=in_specs, out_specs=out_specs, scratch_shapes=scratch),
        out_shape=out_shape,
        compiler_params=pltpu.CompilerParams(
            dimension_semantics=("arbitrary",), vmem_limit_bytes=VMEM_LIMIT),
        name="mixer",
    )(first, last, *x_parts, vglu, vglu, vglu, q, k, k, k, v, v, v, ga, gb,
      conv_w, conv_b, ln_g, ln_b, table, w_a, w_b, w_o, norm2_g, router_w, router_b)


def _ffn_kernel(h_ref, x_ref, w1_ref, w3_ref, w2_ref, *o_refs, o_starts):
    h = h_ref[...]
    acc = x_ref[...]
    for c in range(0, D_FF, FF_CHUNK):
        a = jnp.dot(h, w1_ref[:, c:c + FF_CHUNK], preferred_element_type=F32)
        b = jnp.dot(h, w3_ref[:, c:c + FF_CHUNK], preferred_element_type=F32)
        act = (a * _sigmoid(a) * b).astype(BF16)
        acc = acc + jnp.dot(act, w2_ref[c:c + FF_CHUNK, :], preferred_element_type=F32)
    _write_parts(o_refs, o_starts, acc)


def _out_parts(n_tok, out_rows):
    shapes = [jax.ShapeDtypeStruct((r, D_MODEL), F32) for r in (out_rows or [n_tok])]
    starts, specs, _ = _part_specs(shapes)
    return shapes, starts, specs


def _ffn(h2, x, w1, w3, w2, out_rows=None):
    n_tok = x.shape[0]
    tile = lambda: pl.BlockSpec((TM, D_MODEL), lambda i: (i, 0))
    single = dict(pipeline_mode=pl.Buffered(1))
    o_shapes, o_starts, o_specs = _out_parts(n_tok, out_rows)
    return pl.pallas_call(
        functools.partial(_ffn_kernel, o_starts=o_starts),
        grid=(n_tok // TM,),
        in_specs=[tile(), tile(),
                  pl.BlockSpec((D_MODEL, D_FF), lambda i: (0, 0), **single),
                  pl.BlockSpec((D_MODEL, D_FF), lambda i: (0, 0), **single),
                  pl.BlockSpec((D_FF, D_MODEL), lambda i: (0, 0), **single)],
        out_specs=o_specs,
        out_shape=o_shapes,
        compiler_params=pltpu.CompilerParams(
            dimension_semantics=("arbitrary",), vmem_limit_bytes=VMEM_LIMIT),
        name="ffn",
    )(h2, x, w1, w3, w2)


def _slots_kernel(rt_ref, start_ref, slots_ref, run_ref):
    i = pl.program_id(0)

    @pl.when(i == 0)
    def _():
        run_ref[...] = jnp.zeros(run_ref.shape, F32)

    rt = rt_ref[0]
    eidx = lax.broadcasted_iota(jnp.int32, (N_EXPERTS, TM), 0).astype(F32)
    e1 = eidx == rt[0:1, :]
    e2 = eidx == rt[1:2, :]
    chosen = (e1 | e2).astype(F32)
    before = lax.broadcasted_iota(jnp.int32, (TM, TM), 0) < lax.broadcasted_iota(
        jnp.int32, (TM, TM), 1)
    rank = jnp.dot(chosen.astype(BF16), before.astype(BF16), preferred_element_type=F32)
    pos = rank + start_ref[:, 0:1] + run_ref[:, 0:1]
    s1 = jnp.sum(jnp.where(e1, pos, 0.0), axis=0, keepdims=True)
    s2 = jnp.sum(jnp.where(e2, pos, 0.0), axis=0, keepdims=True)
    slots_ref[0] = jnp.concatenate([s1, s2], axis=0).astype(jnp.int32)
    run_ref[...] += jnp.broadcast_to(jnp.sum(chosen, axis=1, keepdims=True), run_ref.shape)


def _slots(route_t, start):
    n_tiles = route_t.shape[0]
    return pl.pallas_call(
        _slots_kernel,
        grid=(n_tiles,),
        in_specs=[pl.BlockSpec((1, ROUTE_W, TM), lambda i: (i, 0, 0)),
                  pl.BlockSpec((N_EXPERTS, LANES), lambda i: (0, 0))],
        out_specs=pl.BlockSpec((1, 2, TM), lambda i: (i, 0, 0)),
        out_shape=jax.ShapeDtypeStruct((n_tiles, 2, TM), jnp.int32),
        scratch_shapes=[pltpu.VMEM((N_EXPERTS, LANES), F32)],
        compiler_params=pltpu.CompilerParams(dimension_semantics=("arbitrary",)),
        name="moe_slots",
    )(route_t, start)


ROW_DMA_UNROLL = 8


ZERO_ROWS = 64


def _dispatch_kernel(ts_ref, tl_ref, slots_ref, h_ref, xs_ref, zero_ref, sem, zsem):
    i = pl.program_id(0)

    def fill_padding(wait):
        def run(copy):
            copy.wait() if wait else copy.start()

        for e in range(N_EXPERTS):
            first = ts_ref[e]
            head = (-first) & (SUBLANES - 1)
            for r in range(SUBLANES - 1):
                @pl.when(r < head)
                def _():
                    run(pltpu.make_async_copy(
                        zero_ref.at[pl.ds(0, 1)], xs_ref.at[pl.ds(first + r, 1)], zsem))

            base = first + head
            n = tl_ref[e] - head
            full = n // ZERO_ROWS

            def chunk(c, carry):
                off = pl.multiple_of(base + c * ZERO_ROWS, SUBLANES)
                run(pltpu.make_async_copy(zero_ref, xs_ref.at[pl.ds(off, ZERO_ROWS)], zsem))
                return carry

            lax.fori_loop(0, full, chunk, 0)
            rem_base = base + full * ZERO_ROWS
            for b in range(SUBLANES.bit_length() - 1, ZERO_ROWS.bit_length() - 1):
                size = 1 << b

                @pl.when(((n >> b) & 1) == 1)
                def _():
                    off = pl.multiple_of(rem_base + (n & (size - 1)), SUBLANES)
                    run(pltpu.make_async_copy(
                        zero_ref.at[pl.ds(0, size)], xs_ref.at[pl.ds(off, size)], zsem))

    @pl.when(i == 0)
    def _():
        zero_ref[...] = jnp.zeros(zero_ref.shape, F32)
        fill_padding(wait=False)

    def issue(t, carry):
        for c in range(2):
            dst = slots_ref[c * TM + t]
            pltpu.make_async_copy(
                h_ref.at[pl.ds(t, 1)], xs_ref.at[pl.ds(dst, 1)], sem).start(priority=c)
        return carry

    lax.fori_loop(0, TM, issue, 0, unroll=ROW_DMA_UNROLL)
    for c in range(2):
        pltpu.make_async_copy(h_ref, xs_ref.at[pl.ds(0, TM)], sem).wait()

    @pl.when(i == 0)
    def _():
        fill_padding(wait=True)


def _dispatch(tail_start, tail_len, slots_flat, h2, n_rows):
    n_tok = h2.shape[0]
    return pl.pallas_call(
        _dispatch_kernel,
        grid_spec=pltpu.PrefetchScalarGridSpec(
            num_scalar_prefetch=2, grid=(n_tok // TM,),
            in_specs=[pl.BlockSpec((2 * TM,), lambda i, ts, tl: (i,), memory_space=pltpu.SMEM),
                      pl.BlockSpec((TM, D_MODEL), lambda i, ts, tl: (i, 0))],
            out_specs=pl.BlockSpec(memory_space=pl.ANY),
            scratch_shapes=[pltpu.VMEM((ZERO_ROWS, D_MODEL), F32),
                            pltpu.SemaphoreType.DMA, pltpu.SemaphoreType.DMA]),
        out_shape=jax.ShapeDtypeStruct((n_rows, D_MODEL), F32),
        compiler_params=pltpu.CompilerParams(dimension_semantics=("arbitrary",)),
        name="moe_dispatch",
    )(tail_start, tail_len, slots_flat, h2)


def _expert_kernel(te_ref, na_ref, x_ref, w1_ref, w3_ref, w2_ref, y_ref):
    j = pl.program_id(0)

    @pl.when(j < na_ref[0])
    def _():
        h = x_ref[...].astype(BF16)
        acc = jnp.zeros((TM, D_MODEL), F32)
        for c in range(0, D_FF, FF_CHUNK):
            a = jnp.dot(h, w1_ref[0, :, c:c + FF_CHUNK], preferred_element_type=F32)
            b = jnp.dot(h, w3_ref[0, :, c:c + FF_CHUNK], preferred_element_type=F32)
            act = (a * _sigmoid(a) * b).astype(BF16)
            acc = acc + jnp.dot(act, w2_ref[0, c:c + FF_CHUNK, :], preferred_element_type=F32)
        y_ref[...] = acc

    @pl.when(j >= na_ref[0])
    def _():
        y_ref[...] = jnp.zeros(y_ref.shape, F32)


def _experts(tile_expert, n_active, xs, w1, w3, w2):
    n_tiles = xs.shape[0] // TM
    weights = lambda shape: pl.BlockSpec(shape, lambda j, te, na: (te[j], 0, 0))
    return pl.pallas_call(
        _expert_kernel,
        grid_spec=pltpu.PrefetchScalarGridSpec(
            num_scalar_prefetch=2, grid=(n_tiles,),
            in_specs=[pl.BlockSpec((TM, D_MODEL),
                                   lambda j, te, na: (jnp.minimum(j, na[0] - 1), 0)),
                      weights((1, D_MODEL, D_FF)), weights((1, D_MODEL, D_FF)),
                      weights((1, D_FF, D_MODEL))],
            out_specs=pl.BlockSpec((TM, D_MODEL), lambda j, te, na: (j, 0))),
        out_shape=jax.ShapeDtypeStruct(xs.shape, F32),
        compiler_params=pltpu.CompilerParams(
            dimension_semantics=("arbitrary",), vmem_limit_bytes=VMEM_LIMIT),
        name="moe_experts",
    )(tile_expert, n_active, xs, w1, w3, w2)


def _combine_kernel(slots_ref, x_ref, route_ref, y_ref, *refs, o_starts):
    stage_ref, sem = refs[-2:]

    def issue(t, carry):
        for c in range(2):
            src = slots_ref[c * TM + t]
            pltpu.make_async_copy(y_ref.at[pl.ds(src, 1)],
                                  stage_ref.at[pl.ds(c * TM + t, 1)], sem).start(priority=c)
        return carry

    lax.fori_loop(0, TM, issue, 0, unroll=ROW_DMA_UNROLL)
    for c in range(2):
        pltpu.make_async_copy(y_ref.at[pl.ds(0, TM)], stage_ref.at[pl.ds(c * TM, TM)], sem).wait()
    route = route_ref[...]
    _write_parts(refs[:-2], o_starts,
                 x_ref[...] + route[:, 2:3] * stage_ref[0:TM, :]
                 + route[:, 3:4] * stage_ref[TM:2 * TM, :])


def _combine(slots_flat, x, route, y, out_rows=None):
    n_tok = x.shape[0]
    o_shapes, o_starts, o_specs = _out_parts(n_tok, out_rows)
    return pl.pallas_call(
        functools.partial(_combine_kernel, o_starts=o_starts),
        grid=(n_tok // TM,),
        in_specs=[pl.BlockSpec((2 * TM,), lambda i: (i,), memory_space=pltpu.SMEM),
                  pl.BlockSpec((TM, D_MODEL), lambda i: (i, 0)),
                  pl.BlockSpec((TM, ROUTE_W), lambda i: (i, 0)),
                  pl.BlockSpec(memory_space=pl.ANY)],
        out_specs=o_specs,
        out_shape=o_shapes,
        scratch_shapes=[pltpu.VMEM((2 * TM, D_MODEL), F32), pltpu.SemaphoreType.DMA],
        compiler_params=pltpu.CompilerParams(
            dimension_semantics=("arbitrary",), vmem_limit_bytes=VMEM_LIMIT),
        name="moe_combine",
    )(slots_flat, x, route, y)


def _moe(h2, x, route, route_t, totals, w1, w3, w2, out_rows=None):
    n_tok = x.shape[0]
    tot = totals[0, :N_EXPERTS].astype(jnp.int32)
    padded = (tot + TM - 1) // TM * TM
    end = jnp.cumsum(padded)
    start = end - padded
    n_tiles = (2 * n_tok + N_EXPERTS * (TM - 1)) // TM
    n_active = end[-1] // TM
    tile = jnp.minimum(jnp.arange(n_tiles, dtype=jnp.int32), n_active - 1)
    tile_expert = jnp.sum(tile[:, None] * TM >= end[None, :], axis=1).astype(jnp.int32)
    start_b = jnp.broadcast_to(start.astype(F32)[:, None], (N_EXPERTS, LANES))
    tail_start = (start + tot).astype(jnp.int32)
    tail_len = (end.at[-1].set(n_tiles * TM) - tail_start).astype(jnp.int32)

    slots_flat = _slots(route_t, start_b).reshape(-1)
    xs = _dispatch(tail_start, tail_len, slots_flat, h2, n_tiles * TM)
    y = _experts(tile_expert, n_active.reshape(1).astype(jnp.int32), xs, w1, w3, w2)
    return _combine(slots_flat, x, route, y, out_rows)


def _tile_edges(seq_lens):
    first, last = [], []
    for n in seq_lens:
        t = n // TM
        first += [1] + [0] * (t - 1)
        last += [0] * (t - 1) + [1]
    return jnp.asarray(first, jnp.int32), jnp.asarray(last, jnp.int32)


def kernel(x_prompt, x_sample, norm1_g, w_in, b_gate, conv_w, conv_b, conv_ln_g, conv_ln_b,
           q_norm_g, k_norm_g, rpb, w_proj_a, w_proj_b, w_out, norm2_g, ffn_w1, ffn_w3, ffn_w2,
           router_w, router_b, moe_w1, moe_w3, moe_w2):
    depth = w_in.shape[0]
    seq_lens = [x_prompt.shape[1]] * x_prompt.shape[0] + [x_sample.shape[1]] * x_sample.shape[0]
    assert all(n % TM == 0 and n // GRID_W >= 2 * TILE_ROWS for n in seq_lens)
    first, last = _tile_edges(seq_lens)
    x_parts = [x_prompt.reshape(-1, D_MODEL), x_sample.reshape(-1, D_MODEL)]
    out_rows = [p.shape[0] for p in x_parts]

    head_id = jnp.arange(D_ATTN) // HEAD_DIM
    head_mean = jnp.where(head_id[:, None] == head_id[None, :], 1.0 / HEAD_DIM, 0.0).astype(BF16)
    row = lambda a: a.reshape(1, -1).astype(F32)

    for l in range(depth):
        final = out_rows if l == depth - 1 else None
        table = _bias_table(rpb[l])
        vglu, q, k, v, ga, gb = _in_proj(
            x_parts, row(norm1_g[l]), _to_bf16(w_in[l]), row(b_gate[l]),
            row(jnp.tile(q_norm_g[l], N_HEADS)), row(jnp.tile(k_norm_g[l], N_HEADS)), head_mean)
        j = l // 2
        moe = l % 2 == 1
        if moe:
            rw = jnp.pad(router_w[j], ((0, 0), (0, LANES - N_EXPERTS))).astype(BF16)
            rb = jnp.pad(row(router_b[j]), ((0, 0), (0, LANES - N_EXPERTS)))
        else:
            rw = jnp.zeros((D_MODEL, LANES), BF16)
            rb = jnp.zeros((1, LANES), F32)
        cw = jnp.pad(conv_w[l], ((0, 1), (0, 0)))
        outs = _mixer(
            x_parts, vglu, q, k, v, ga, gb, first, last, cw, row(conv_b[l]), row(conv_ln_g[l]),
            row(conv_ln_b[l]), table, _to_bf16(w_proj_a[l]), _to_bf16(w_proj_b[l]),
            _to_bf16(w_out[l]), row(norm2_g[l]), rw, rb, with_router=moe)
        if moe:
            x, h2, route, route_t, totals = outs
            res = _moe(h2, x, route, route_t, totals, _to_bf16(moe_w1[j]), _to_bf16(moe_w3[j]),
                       _to_bf16(moe_w2[j]), final)
        else:
            x, h2 = outs
            res = _ffn(h2, x, _to_bf16(ffn_w1[j]), _to_bf16(ffn_w3[j]), _to_bf16(ffn_w2[j]),
                       final)
        x_parts = list(res)

    return (x_parts[0].reshape(x_prompt.shape), x_parts[1].reshape(x_sample.shape))
```

```python
import functools

import jax
import jax.numpy as jnp
from jax import lax
from jax.experimental import pallas as pl
from jax.experimental.pallas import tpu as pltpu

D_MODEL = 1024
GRID_W = 64
D_CONV = 512
CONV_W = 31
N_HEADS = 8
HEAD_DIM = 64
D_ATTN = N_HEADS * HEAD_DIM
MAX_WIN_H = 8
WIN_W = 16
D_FF = 2816
N_EXPERTS = 8
EPS = 1e-6

LANES = 128
SUBLANES = 8
TM = 512
TILE_ROWS = TM // GRID_W
KV_HALO = 256
CONV_HALO = 16
CONV_PAD = CONV_W // 2
N_RO = 2 * MAX_WIN_H - 1
N_CO = 2 * WIN_W - 1
HEAD_GROUP = 4
ROUTE_W = 8
ATT_UNROLL = 8
NEG_BIG = -1e30
LOG2E = 1.4426950408889634
FF_CHUNK = 256
VMEM_LIMIT = 56 * 1024 * 1024

F32 = jnp.float32
BF16 = jnp.bfloat16


def _sigmoid(x):
    return 1.0 / (1.0 + jnp.exp(-x))


def _round_up_rows(count):
    return jnp.floor((count + (SUBLANES - 1)) * (1.0 / SUBLANES)) * SUBLANES


def _part_specs(parts, width=D_MODEL):
    starts, specs, tile0 = [], [], 0
    for a in parts:
        n = a.shape[0] // TM
        specs.append(pl.BlockSpec(
            (TM, width), lambda i, *_, s=tile0, n=n: (jnp.clip(i - s, 0, n - 1), 0)))
        starts.append(tile0)
        tile0 += n
    return tuple(starts), specs, tile0


def _read_parts(refs, starts):
    i = pl.program_id(0)
    x = refs[0][...]
    for ref, s in zip(refs[1:], starts[1:]):
        x = jnp.where(i >= s, ref[...], x)
    return x


def _write_parts(refs, starts, value):
    i = pl.program_id(0)
    ends = starts[1:] + (None,)
    for ref, s, e in zip(refs, starts, ends):
        owns = i >= s if e is None else (i >= s) & (i < e)

        @pl.when(owns)
        def _():
            ref[...] = value


CAST_ROWS = 512
CAST_BLOCK_ELEMS = 3 * 1024 * 1024


def _cast_kernel(w_ref, o_ref):
    o_ref[...] = w_ref[...].astype(BF16)


def _to_bf16(w, take=None):
    lead = w.shape[:-2]
    k, n = w.shape[-2:]
    w3 = w.reshape((-1, k, n))
    n_mat = w3.shape[0] if take is None else 1
    first = 0 if take is None else take
    rows = k if k * n <= CAST_BLOCK_ELEMS else next(
        (r for r in (CAST_ROWS, CAST_ROWS // 2, CAST_ROWS // 4) if k % r == 0), k)
    out = pl.pallas_call(
        _cast_kernel,
        grid=(n_mat, k // rows),
        in_specs=[pl.BlockSpec((1, rows, n), lambda e, r: (first + e, r, 0))],
        out_specs=pl.BlockSpec((1, rows, n), lambda e, r: (e, r, 0)),
        out_shape=jax.ShapeDtypeStruct((n_mat, k, n), BF16),
        compiler_params=pltpu.CompilerParams(
            dimension_semantics=("arbitrary", "arbitrary"), vmem_limit_bytes=VMEM_LIMIT),
        name="to_bf16",
    )(w3)
    return out.reshape((k, n) if take is not None else lead + (k, n))


def _bias_table_kernel(rpb_ref, out_ref):
    ro = pl.program_id(0)
    shape = (GRID_W, HEAD_GROUP * GRID_W)
    kcol = lax.broadcasted_iota(jnp.int32, shape, 0)
    lane = lax.broadcasted_iota(jnp.int32, shape, 1)
    head_local = lane // GRID_W
    qcol = lane - head_local * GRID_W
    start_c = jnp.clip(qcol - WIN_W // 2, 0, GRID_W - WIN_W)
    valid = (kcol >= start_c) & (kcol < start_c + WIN_W)
    co = jnp.clip(kcol - qcol + (WIN_W - 1), 0, 2 * WIN_W - 2)
    for g in range(N_HEADS // HEAD_GROUP):
        acc = jnp.zeros(shape, F32)
        for d in range(N_CO):
            val = jnp.zeros(shape, F32)
            for hl in range(HEAD_GROUP):
                h = g * HEAD_GROUP + hl
                val = jnp.where(head_local == hl, rpb_ref[(h * N_RO + ro) * N_CO + d], val)
            acc = jnp.where(co == d, val, acc)
        out_ref[0, g] = jnp.where(valid, acc * LOG2E, NEG_BIG)


def _bias_table(rpb_l):
    n_groups = N_HEADS // HEAD_GROUP
    width = HEAD_GROUP * GRID_W
    return pl.pallas_call(
        _bias_table_kernel,
        grid=(N_RO,),
        in_specs=[pl.BlockSpec(memory_space=pltpu.SMEM)],
        out_specs=pl.BlockSpec((1, n_groups, GRID_W, width), lambda r: (r, 0, 0, 0)),
        out_shape=jax.ShapeDtypeStruct((N_RO, n_groups, GRID_W, width), F32),
        name="bias_table",
    )(rpb_l.reshape(-1))


def _in_proj_kernel(*refs, x_starts):
    n_parts = len(x_starts)
    (g_ref, w_ref, bg_ref, qg_ref, kg_ref, hm_ref,
     vglu_ref, q_ref, k_ref, v_ref, ga_ref, gb_ref) = refs[n_parts:]
    x = _read_parts(refs[:n_parts], x_starts)
    ms = jnp.mean(x * x, axis=-1, keepdims=True)
    h = (x * lax.rsqrt(ms + EPS) * g_ref[...]).astype(BF16)

    def proj(lo, hi):
        return jnp.dot(h, w_ref[:, lo:hi], preferred_element_type=F32)

    def head_norm(t, gain):
        msq = jnp.dot((t * t).astype(BF16), hm_ref[...], preferred_element_type=F32)
        return t * lax.rsqrt(msq + EPS) * gain

    a = proj(0, D_CONV)
    gate = proj(D_CONV, 2 * D_CONV)
    vglu_ref[...] = (a * _sigmoid(gate)).astype(BF16)
    o = 2 * D_CONV
    q_ref[...] = (head_norm(proj(o, o + D_ATTN), qg_ref[...])
                  * (HEAD_DIM ** -0.5 * LOG2E)).astype(BF16)
    k_ref[...] = head_norm(proj(o + D_ATTN, o + 2 * D_ATTN), kg_ref[...]).astype(BF16)
    v_ref[...] = proj(o + 2 * D_ATTN, o + 3 * D_ATTN).astype(BF16)
    o += 3 * D_ATTN
    bg = bg_ref[...]
    ga_ref[...] = _sigmoid(proj(o, o + D_MODEL) + bg[:, :D_MODEL]).astype(BF16)
    gb_ref[...] = _sigmoid(proj(o + D_MODEL, o + 2 * D_MODEL) + bg[:, D_MODEL:]).astype(BF16)


def _in_proj(x_parts, norm_g, w_in, b_gate, q_norm_g, k_norm_g, head_mean):
    x_starts, x_specs, n_tiles = _part_specs(x_parts)
    n_tok = n_tiles * TM
    d_in = w_in.shape[1]
    full = lambda shape: pl.BlockSpec(shape, lambda i: (0, 0))
    tile = lambda width: pl.BlockSpec((TM, width), lambda i: (i, 0))
    out = lambda width: jax.ShapeDtypeStruct((n_tok, width), BF16)
    return pl.pallas_call(
        functools.partial(_in_proj_kernel, x_starts=x_starts),
        grid=(n_tiles,),
        in_specs=x_specs + [full((1, D_MODEL)), full((D_MODEL, d_in)),
                            full((1, 2 * D_MODEL)), full((1, D_ATTN)), full((1, D_ATTN)),
                            full((D_ATTN, D_ATTN))],
        out_specs=[tile(D_CONV), tile(D_ATTN), tile(D_ATTN), tile(D_ATTN),
                   tile(D_MODEL), tile(D_MODEL)],
        out_shape=[out(D_CONV), out(D_ATTN), out(D_ATTN), out(D_ATTN),
                   out(D_MODEL), out(D_MODEL)],
        compiler_params=pltpu.CompilerParams(
            dimension_semantics=("arbitrary",), vmem_limit_bytes=VMEM_LIMIT),
        name="in_proj",
    )(*x_parts, norm_g, w_in, b_gate, q_norm_g, k_norm_g, head_mean)


def _mixer_kernel(first_ref, last_ref, *refs, x_starts, with_router):
    n_parts = len(x_starts)
    (vg_ref, vgp_ref, vgn_ref, q_ref,
     k_ref, kp_ref, kn_ref, v_ref, vp_ref, vn_ref,
     ga_ref, gb_ref, cw_ref, cb_ref, lg_ref, lb_ref, tab_ref,
     wa_ref, wb_ref, wo_ref, n2_ref, rw_ref, rb_ref) = refs[n_parts:n_parts + 23]
    rest = refs[n_parts + 23:]
    if with_router:
        xo_ref, h2_ref, route_ref, route_t_ref, tot_ref = rest[:5]
    else:
        xo_ref, h2_ref = rest[:2]
    ext_ref, sh_ref, conv_ref, kext_ref, vext_ref, att_ref = rest[-6:]
    i = pl.program_id(0)
    is_first = first_ref[i]
    is_last = last_ref[i]

    keep_prev = (1 - is_first).astype(F32)
    keep_next = (1 - is_last).astype(F32)
    ext_ref[0:CONV_HALO, :] = vgp_ref[...].astype(F32) * keep_prev
    ext_ref[CONV_HALO:CONV_HALO + TM, :] = vg_ref[...].astype(F32)
    ext_ref[CONV_HALO + TM:, :] = vgn_ref[...].astype(F32) * keep_next
    n_sh_rows = sh_ref.shape[1]
    row_chunk = 128
    for cb in range(D_CONV // LANES):
        cols = slice(cb * LANES, (cb + 1) * LANES)
        for b in range(8):
            sh_ref[b] = ext_ref[b:b + n_sh_rows, cols]
        def conv_rows(rc, carry, cols=cols):
            base = pl.multiple_of(rc * row_chunk, row_chunk)
            acc = jnp.zeros((row_chunk, LANES), F32)
            for j in range(CONV_W):
                off = CONV_HALO - CONV_PAD + j
                a8, b = off // SUBLANES, off % SUBLANES
                acc = acc + (sh_ref[b, pl.ds(base + SUBLANES * a8, row_chunk), :]
                             * cw_ref[j:j + 1, cols])
            conv_ref[pl.ds(base, row_chunk), cols] = acc
            return carry

        lax.fori_loop(0, TM // row_chunk, conv_rows, 0)
    y = conv_ref[...] + cb_ref[...]
    mu = jnp.mean(y, axis=-1, keepdims=True)
    yc = y - mu
    var = jnp.mean(yc * yc, axis=-1, keepdims=True)
    yn = yc * lax.rsqrt(var + EPS) * lg_ref[...] + lb_ref[...]
    act = (yn * _sigmoid(yn)).astype(BF16)
    br_a = jnp.dot(act, wa_ref[...], preferred_element_type=F32)

    kext_ref[0:KV_HALO, :] = kp_ref[...]
    kext_ref[KV_HALO:KV_HALO + TM, :] = k_ref[...]
    kext_ref[KV_HALO + TM:, :] = kn_ref[...]
    vext_ref[0:KV_HALO, :] = vp_ref[...]
    vext_ref[KV_HALO:KV_HALO + TM, :] = v_ref[...]
    vext_ref[KV_HALO + TM:, :] = vn_ref[...]
    gw = HEAD_GROUP * HEAD_DIM
    lane_head = lax.broadcasted_iota(jnp.int32, (GRID_W, gw), 1) // HEAD_DIM
    win = MAX_WIN_H * GRID_W

    def row_body(j, carry):
        ws = j - MAX_WIN_H // 2
        ws = jnp.where(is_first == 1, jnp.maximum(ws, 0), ws)
        ws = jnp.where(is_last == 1, jnp.minimum(ws, 0), ws)
        ro0 = ws - j + (MAX_WIN_H - 1)
        kstart = pl.multiple_of((ws + KV_HALO // GRID_W) * GRID_W, GRID_W)
        qstart = pl.multiple_of(j * GRID_W, GRID_W)
        for g in range(N_HEADS // HEAD_GROUP):
            cols = slice(g * gw, (g + 1) * gw)
            q4 = q_ref[pl.ds(qstart, GRID_W), cols]
            qexp = jnp.concatenate(
                [jnp.where(lane_head == hl, q4, jnp.zeros_like(q4)) for hl in range(HEAD_GROUP)],
                axis=0)
            kslab = kext_ref[pl.ds(kstart, win), cols]
            st = lax.dot_general(kslab, qexp, (((1,), (1,)), ((), ())),
                                 preferred_element_type=F32)
            st = st + jnp.concatenate([tab_ref[ro0 + w, g] for w in range(MAX_WIN_H)], axis=0)
            m = jnp.max(st, axis=0, keepdims=True)
            pexp = jnp.exp2(st - m)
            inv = 1.0 / jnp.sum(pexp, axis=0, keepdims=True)
            pn = (pexp * inv).astype(BF16)
            vslab = vext_ref[pl.ds(kstart, win), cols]
            o4 = lax.dot_general(pn, vslab, (((0,), (0,)), ((), ())),
                                 preferred_element_type=F32)
            out = o4[0:GRID_W]
            for hl in range(1, HEAD_GROUP):
                out = jnp.where(lane_head == hl, o4[hl * GRID_W:(hl + 1) * GRID_W], out)
            att_ref[pl.ds(qstart, GRID_W), cols] = out
        return carry

    lax.fori_loop(0, TILE_ROWS, row_body, 0, unroll=ATT_UNROLL)
    br_b = jnp.dot(att_ref[...].astype(BF16), wb_ref[...], preferred_element_type=F32)

    merged = ga_ref[...].astype(F32) * br_a + gb_ref[...].astype(F32) * br_b
    x_new = (_read_parts(refs[:n_parts], x_starts)
             + jnp.dot(merged.astype(BF16), wo_ref[...], preferred_element_type=F32))
    xo_ref[...] = x_new
    ms = jnp.mean(x_new * x_new, axis=-1, keepdims=True)
    h2 = x_new * lax.rsqrt(ms + EPS) * n2_ref[...]
    h2_ref[...] = h2.astype(h2_ref.dtype)

    if with_router:
        logits = jnp.dot(h2.astype(BF16), rw_ref[...], preferred_element_type=F32) + rb_ref[...]
        elane = lax.broadcasted_iota(jnp.int32, logits.shape, 1).astype(F32)
        logits = jnp.where(elane < N_EXPERTS, logits, -jnp.inf)
        m1 = jnp.max(logits, axis=1, keepdims=True)
        i1 = jnp.min(jnp.where(logits == m1, elane, float(LANES)), axis=1, keepdims=True)
        others = jnp.where(elane == i1, -jnp.inf, logits)
        m2 = jnp.max(others, axis=1, keepdims=True)
        i2 = jnp.min(jnp.where(others == m2, elane, float(LANES)), axis=1, keepdims=True)
        e2 = jnp.exp(m2 - m1)
        w1 = 1.0 / (1.0 + e2)
        w2 = e2 / (1.0 + e2)
        route = jnp.where(elane == 0, i1, jnp.where(elane == 1, i2, jnp.where(
            elane == 2, w1, jnp.where(elane == 3, w2, 0.0))))
        route_ref[...] = route[:, :ROUTE_W]
        route_t_ref[0] = route.T[:ROUTE_W, :]
        onehot = ((elane == i1) | (elane == i2)).astype(F32)

        @pl.when(i == 0)
        def _():
            tot_ref[...] = jnp.zeros(tot_ref.shape, F32)

        tot_ref[...] += jnp.broadcast_to(
            _round_up_rows(jnp.sum(onehot, axis=0, keepdims=True)), tot_ref.shape)


def _mixer(x_parts, vglu, q, k, v, ga, gb, first, last, conv_w, conv_b, ln_g, ln_b, table,
           w_a, w_b, w_o, norm2_g, router_w, router_b, *, with_router):
    x_starts, x_specs, n_tiles = _part_specs(x_parts)
    n_tok = n_tiles * TM
    conv_blocks = TM // CONV_HALO
    kv_blocks = TM // KV_HALO
    full2 = lambda shape: pl.BlockSpec(shape, lambda i, f, l: (0, 0))
    tile = lambda width: pl.BlockSpec((TM, width), lambda i, f, l: (i, 0))
    in_specs = x_specs + [
        tile(D_CONV),
        pl.BlockSpec((CONV_HALO, D_CONV), lambda i, f, l: (i * conv_blocks - 1 + f[i], 0)),
        pl.BlockSpec((CONV_HALO, D_CONV), lambda i, f, l: ((i + 1) * conv_blocks - l[i], 0)),
        tile(D_ATTN),
        tile(D_ATTN),
        pl.BlockSpec((KV_HALO, D_ATTN), lambda i, f, l: (i * kv_blocks - 1 + f[i], 0)),
        pl.BlockSpec((KV_HALO, D_ATTN), lambda i, f, l: ((i + 1) * kv_blocks - l[i], 0)),
        tile(D_ATTN),
        pl.BlockSpec((KV_HALO, D_ATTN), lambda i, f, l: (i * kv_blocks - 1 + f[i], 0)),
        pl.BlockSpec((KV_HALO, D_ATTN), lambda i, f, l: ((i + 1) * kv_blocks - l[i], 0)),
        tile(D_MODEL),
        tile(D_MODEL),
        full2((CONV_W + 1, D_CONV)),
        full2((1, D_CONV)), full2((1, D_CONV)), full2((1, D_CONV)),
        pl.BlockSpec(table.shape, lambda i, f, l: (0, 0, 0, 0)),
        full2((D_CONV, D_MODEL)), full2((D_ATTN, D_MODEL)), full2((D_MODEL, D_MODEL)),
        full2((1, D_MODEL)),
        full2((D_MODEL, LANES)), full2((1, LANES)),
    ]
    out_specs = [tile(D_MODEL), tile(D_MODEL)]
    out_shape = [jax.ShapeDtypeStruct((n_tok, D_MODEL), F32),
                 jax.ShapeDtypeStruct((n_tok, D_MODEL), F32 if with_router else BF16)]
    if with_router:
        out_specs += [tile(ROUTE_W),
                      pl.BlockSpec((1, ROUTE_W, TM), lambda i, f, l: (i, 0, 0)),
                      pl.BlockSpec((8, LANES), lambda i, f, l: (0, 0))]
        out_shape += [jax.ShapeDtypeStruct((n_tok, ROUTE_W), F32),
                      jax.ShapeDtypeStruct((n_tok // TM, ROUTE_W, TM), F32),
                      jax.ShapeDtypeStruct((8, LANES), F32)]
    ext_rows = TM + 2 * CONV_HALO
    scratch = [
        pltpu.VMEM((ext_rows, D_CONV), F32),
        pltpu.VMEM((8, ext_rows - 8, LANES), F32),
        pltpu.VMEM((TM, D_CONV), F32),
        pltpu.VMEM((TM + 2 * KV_HALO, D_ATTN), BF16),
        pltpu.VMEM((TM + 2 * KV_HALO, D_ATTN), BF16),
        pltpu.VMEM((TM, D_ATTN), F32),
    ]
    return pl.pallas_call(
        functools.partial(_mixer_kernel, x_starts=x_starts, with_router=with_router),
        grid_spec=pltpu.PrefetchScalarGridSpec(
            num_scalar_prefetch=2, grid=(n_tiles,),
            in_specs=in_specs, out_specs=out_specs, scratch_shapes=scratch),
        out_shape=out_shape,
        compiler_params=pltpu.CompilerParams(
            dimension_semantics=("arbitrary",), vmem_limit_bytes=VMEM_LIMIT),
        name="mixer",
    )(first, last, *x_parts, vglu, vglu, vglu, q, k, k, k, v, v, v, ga, gb,
      conv_w, conv_b, ln_g, ln_b, table, w_a, w_b, w_o, norm2_g, router_w, router_b)


def _ffn_kernel(h_ref, x_ref, w1_ref, w3_ref, w2_ref, *o_refs, o_starts):
    h = h_ref[...]
    acc = x_ref[...]
    for c in range(0, D_FF, FF_CHUNK):
        a = jnp.dot(h, w1_ref[:, c:c + FF_CHUNK], preferred_element_type=F32)
        b = jnp.dot(h, w3_ref[:, c:c + FF_CHUNK], preferred_element_type=F32)
        act = (a * _sigmoid(a) * b).astype(BF16)
        acc = acc + jnp.dot(act, w2_ref[c:c + FF_CHUNK, :], preferred_element_type=F32)
    _write_parts(o_refs, o_starts, acc)


def _out_parts(n_tok, out_rows):
    shapes = [jax.ShapeDtypeStruct((r, D_MODEL), F32) for r in (out_rows or [n_tok])]
    starts, specs, _ = _part_specs(shapes)
    return shapes, starts, specs


def _ffn(h2, x, w1, w3, w2, out_rows=None):
    n_tok = x.shape[0]
    tile = lambda: pl.BlockSpec((TM, D_MODEL), lambda i: (i, 0))
    single = dict(pipeline_mode=pl.Buffered(1))
    o_shapes, o_starts, o_specs = _out_parts(n_tok, out_rows)
    return pl.pallas_call(
        functools.partial(_ffn_kernel, o_starts=o_starts),
        grid=(n_tok // TM,),
        in_specs=[tile(), tile(),
                  pl.BlockSpec((D_MODEL, D_FF), lambda i: (0, 0), **single),
                  pl.BlockSpec((D_MODEL, D_FF), lambda i: (0, 0), **single),
                  pl.BlockSpec((D_FF, D_MODEL), lambda i: (0, 0), **single)],
        out_specs=o_specs,
        out_shape=o_shapes,
        compiler_params=pltpu.CompilerParams(
            dimension_semantics=("arbitrary",), vmem_limit_bytes=VMEM_LIMIT),
        name="ffn",
    )(h2, x, w1, w3, w2)


RUN_FIELDS = 3
PERM_ROWS = 2 * TM + N_EXPERTS * SUBLANES


def _slots_kernel(rt_ref, start_ref, loc_ref, runs_ref, run_ref):
    i = pl.program_id(0)

    @pl.when(i == 0)
    def _():
        run_ref[...] = jnp.zeros(run_ref.shape, F32)

    rt = rt_ref[0]
    eidx = lax.broadcasted_iota(jnp.int32, (N_EXPERTS, TM), 0).astype(F32)
    e1 = eidx == rt[0:1, :]
    e2 = eidx == rt[1:2, :]
    chosen = (e1 | e2).astype(F32)
    before = lax.broadcasted_iota(jnp.int32, (TM, TM), 0) < lax.broadcasted_iota(
        jnp.int32, (TM, TM), 1)
    rank = jnp.dot(chosen.astype(BF16), before.astype(BF16), preferred_element_type=F32)
    rows = _round_up_rows(jnp.sum(chosen, axis=1, keepdims=True))
    rows_b = jnp.broadcast_to(rows, (N_EXPERTS, LANES))
    expert = lax.broadcasted_iota(jnp.int32, (N_EXPERTS, LANES), 0)
    incl = rows_b
    for step in (1, 2, 4):
        incl = incl + jnp.where(expert >= step, pltpu.roll(incl, step, axis=0), 0.0)
    local = incl - rows_b
    pos = rank + local[:, 0:1]
    l1 = jnp.sum(jnp.where(e1, pos, 0.0), axis=0, keepdims=True)
    l2 = jnp.sum(jnp.where(e2, pos, 0.0), axis=0, keepdims=True)
    loc_ref[0] = jnp.concatenate([l1, l2], axis=0).astype(jnp.int32)
    lane = lax.broadcasted_iota(jnp.int32, (N_EXPERTS, LANES), 1)
    dest = start_ref[...] + run_ref[...]
    runs_ref[0] = jnp.where(lane == 0, dest, jnp.where(lane == 1, rows_b, local)).astype(jnp.int32)
    run_ref[...] += rows_b


def _slots(route_t, start):
    n_tiles = route_t.shape[0]
    return pl.pallas_call(
        _slots_kernel,
        grid=(n_tiles,),
        in_specs=[pl.BlockSpec((1, ROUTE_W, TM), lambda i: (i, 0, 0)),
                  pl.BlockSpec((N_EXPERTS, LANES), lambda i: (0, 0))],
        out_specs=[pl.BlockSpec((1, 2, TM), lambda i: (i, 0, 0)),
                   pl.BlockSpec((1, N_EXPERTS, LANES), lambda i: (i, 0, 0))],
        out_shape=[jax.ShapeDtypeStruct((n_tiles, 2, TM), jnp.int32),
                   jax.ShapeDtypeStruct((n_tiles, N_EXPERTS, LANES), jnp.int32)],
        scratch_shapes=[pltpu.VMEM((N_EXPERTS, LANES), F32)],
        compiler_params=pltpu.CompilerParams(dimension_semantics=("arbitrary",)),
        name="moe_slots",
    )(route_t, start)


def _run_copies(runs_ref, tile, make_copy, wait):
    for e in range(N_EXPERTS):
        base = (tile * N_EXPERTS + e) * RUN_FIELDS
        sorted_row, n, local_row = runs_ref[base], runs_ref[base + 1], runs_ref[base + 2]
        for b in range(SUBLANES.bit_length() - 1, TM.bit_length()):
            size = 1 << b

            @pl.when(((n >> b) & 1) == 1)
            def _():
                done = n & (size - 1)
                copy = make_copy(pl.multiple_of(sorted_row + done, SUBLANES),
                                 pl.multiple_of(local_row + done, SUBLANES), size)
                copy.wait() if wait else copy.start()


def _perm_matrix(loc_row):
    rows = lax.broadcasted_iota(jnp.int32, (PERM_ROWS, TM), 0)
    return rows == loc_row


ROW_DMA_UNROLL = 8


ZERO_ROWS = 64


def _dispatch_kernel(ts_ref, tl_ref, runs_ref, loc_ref, h_ref, xs_ref,
                     zero_ref, grouped_ref, sem, zsem):
    i = pl.program_id(0)

    def fill_padding(wait):
        def run(copy):
            copy.wait() if wait else copy.start()

        for e in range(N_EXPERTS):
            base = ts_ref[e]
            n = tl_ref[e]
            full = n // ZERO_ROWS

            def chunk(c, carry):
                off = pl.multiple_of(base + c * ZERO_ROWS, SUBLANES)
                run(pltpu.make_async_copy(zero_ref, xs_ref.at[pl.ds(off, ZERO_ROWS)], zsem))
                return carry

            lax.fori_loop(0, full, chunk, 0)
            rem_base = base + full * ZERO_ROWS
            for b in range(SUBLANES.bit_length() - 1, ZERO_ROWS.bit_length() - 1):
                size = 1 << b

                @pl.when(((n >> b) & 1) == 1)
                def _():
                    off = pl.multiple_of(rem_base + (n & (size - 1)), SUBLANES)
                    run(pltpu.make_async_copy(
                        zero_ref.at[pl.ds(0, size)], xs_ref.at[pl.ds(off, size)], zsem))

    @pl.when(i == 0)
    def _():
        zero_ref[...] = jnp.zeros(zero_ref.shape, F32)
        fill_padding(wait=False)

    slot = lax.rem(i, 2)
    loc = loc_ref[0]
    perm = (_perm_matrix(loc[0:1, :]) | _perm_matrix(loc[1:2, :])).astype(BF16)
    grouped_ref[slot] = jnp.dot(perm, h_ref[...].astype(BF16), preferred_element_type=F32)

    def run_copy(buf):
        def make(sorted_row, local_row, size):
            return pltpu.make_async_copy(grouped_ref.at[buf, pl.ds(local_row, size)],
                                         xs_ref.at[pl.ds(sorted_row, size)], sem.at[buf])
        return make

    _run_copies(runs_ref, i, run_copy(slot), wait=False)

    @pl.when(i > 0)
    def _():
        _run_copies(runs_ref, i - 1, run_copy(1 - slot), wait=True)

    @pl.when(i == pl.num_programs(0) - 1)
    def _():
        _run_copies(runs_ref, i, run_copy(slot), wait=True)

    @pl.when(i == 0)
    def _():
        fill_padding(wait=True)


def _dispatch(tail_start, tail_len, runs_flat, loc, h2, n_rows):
    n_tok = h2.shape[0]
    return pl.pallas_call(
        _dispatch_kernel,
        grid_spec=pltpu.PrefetchScalarGridSpec(
            num_scalar_prefetch=3, grid=(n_tok // TM,),
            in_specs=[pl.BlockSpec((1, 2, TM), lambda i, *_: (i, 0, 0)),
                      pl.BlockSpec((TM, D_MODEL), lambda i, *_: (i, 0))],
            out_specs=pl.BlockSpec(memory_space=pl.ANY),
            scratch_shapes=[pltpu.VMEM((ZERO_ROWS, D_MODEL), F32),
                            pltpu.VMEM((2, PERM_ROWS, D_MODEL), F32),
                            pltpu.SemaphoreType.DMA((2,)), pltpu.SemaphoreType.DMA]),
        out_shape=jax.ShapeDtypeStruct((n_rows, D_MODEL), F32),
        compiler_params=pltpu.CompilerParams(
            dimension_semantics=("arbitrary",), vmem_limit_bytes=VMEM_LIMIT),
        name="moe_dispatch",
    )(tail_start, tail_len, runs_flat, loc, h2)


def _expert_kernel(te_ref, na_ref, x_ref, w1_ref, w3_ref, w2_ref, y_ref):
    j = pl.program_id(0)

    @pl.when(j < na_ref[0])
    def _():
        h = x_ref[...].astype(BF16)
        acc = jnp.zeros((TM, D_MODEL), F32)
        for c in range(0, D_FF, FF_CHUNK):
            a = jnp.dot(h, w1_ref[0, :, c:c + FF_CHUNK], preferred_element_type=F32)
            b = jnp.dot(h, w3_ref[0, :, c:c + FF_CHUNK], preferred_element_type=F32)
            act = (a * _sigmoid(a) * b).astype(BF16)
            acc = acc + jnp.dot(act, w2_ref[0, c:c + FF_CHUNK, :], preferred_element_type=F32)
        y_ref[...] = acc

    @pl.when(j >= na_ref[0])
    def _():
        y_ref[...] = jnp.zeros(y_ref.shape, F32)


def _experts(tile_expert, n_active, xs, w1, w3, w2):
    n_tiles = xs.shape[0] // TM
    weights = lambda shape: pl.BlockSpec(shape, lambda j, te, na: (te[j], 0, 0))
    return pl.pallas_call(
        _expert_kernel,
        grid_spec=pltpu.PrefetchScalarGridSpec(
            num_scalar_prefetch=2, grid=(n_tiles,),
            in_specs=[pl.BlockSpec((TM, D_MODEL),
                                   lambda j, te, na: (jnp.minimum(j, na[0] - 1), 0)),
                      weights((1, D_MODEL, D_FF)), weights((1, D_MODEL, D_FF)),
                      weights((1, D_FF, D_MODEL))],
            out_specs=pl.BlockSpec((TM, D_MODEL), lambda j, te, na: (j, 0))),
        out_shape=jax.ShapeDtypeStruct(xs.shape, F32),
        compiler_params=pltpu.CompilerParams(
            dimension_semantics=("arbitrary",), vmem_limit_bytes=VMEM_LIMIT),
        name="moe_experts",
    )(tile_expert, n_active, xs, w1, w3, w2)


def _combine_kernel(runs_ref, loc_ref, x_ref, route_ref, y_ref, *refs, o_starts):
    stage_ref, sem = refs[-2:]
    i = pl.program_id(0)

    slot = lax.rem(i, 2)

    def run_copy(buf):
        def make(sorted_row, local_row, size):
            return pltpu.make_async_copy(y_ref.at[pl.ds(sorted_row, size)],
                                         stage_ref.at[buf, pl.ds(local_row, size)], sem.at[buf])
        return make

    @pl.when(i == 0)
    def _():
        stage_ref[...] = jnp.zeros(stage_ref.shape, F32)
        _run_copies(runs_ref, i, run_copy(slot), wait=False)

    @pl.when(i + 1 < pl.num_programs(0))
    def _():
        _run_copies(runs_ref, i + 1, run_copy(1 - slot), wait=False)

    _run_copies(runs_ref, i, run_copy(slot), wait=True)
    grouped = stage_ref[slot].astype(BF16)
    loc = loc_ref[0]
    route = route_ref[...]
    out = x_ref[...]
    for c in range(2):
        picked = lax.dot_general(_perm_matrix(loc[c:c + 1, :]).astype(BF16), grouped,
                                 (((0,), (0,)), ((), ())), preferred_element_type=F32)
        out = out + route[:, 2 + c:3 + c] * picked
    _write_parts(refs[:-2], o_starts, out)


def _combine(runs_flat, loc, x, route, y, out_rows=None):
    n_tok = x.shape[0]
    o_shapes, o_starts, o_specs = _out_parts(n_tok, out_rows)
    return pl.pallas_call(
        functools.partial(_combine_kernel, o_starts=o_starts),
        grid_spec=pltpu.PrefetchScalarGridSpec(
            num_scalar_prefetch=1, grid=(n_tok // TM,),
            in_specs=[pl.BlockSpec((1, 2, TM), lambda i, *_: (i, 0, 0)),
                      pl.BlockSpec((TM, D_MODEL), lambda i, *_: (i, 0)),
                      pl.BlockSpec((TM, ROUTE_W), lambda i, *_: (i, 0)),
                      pl.BlockSpec(memory_space=pl.ANY)],
            out_specs=o_specs,
            scratch_shapes=[pltpu.VMEM((2, PERM_ROWS, D_MODEL), F32),
                            pltpu.SemaphoreType.DMA((2,))]),
        out_shape=o_shapes,
        compiler_params=pltpu.CompilerParams(
            dimension_semantics=("arbitrary",), vmem_limit_bytes=VMEM_LIMIT),
        name="moe_combine",
    )(runs_flat, loc, x, route, y)


def _moe(h2, x, route, route_t, totals, w1, w3, w2, out_rows=None):
    n_tok = x.shape[0]
    tot = totals[0, :N_EXPERTS].astype(jnp.int32)
    padded = (tot + TM - 1) // TM * TM
    end = jnp.cumsum(padded)
    start = end - padded
    max_rows = 2 * n_tok + (n_tok // TM) * N_EXPERTS * (SUBLANES - 1) + N_EXPERTS * (TM - 1)
    n_tiles = max_rows // TM
    n_active = end[-1] // TM
    tile = jnp.minimum(jnp.arange(n_tiles, dtype=jnp.int32), n_active - 1)
    tile_expert = jnp.sum(tile[:, None] * TM >= end[None, :], axis=1).astype(jnp.int32)
    start_b = jnp.broadcast_to(start.astype(F32)[:, None], (N_EXPERTS, LANES))
    tail_start = (start + tot).astype(jnp.int32)
    tail_len = (end.at[-1].set(n_tiles * TM) - tail_start).astype(jnp.int32)

    loc, runs = _slots(route_t, start_b)
    runs_flat = runs[:, :, :RUN_FIELDS].reshape(-1)
    xs = _dispatch(tail_start, tail_len, runs_flat, loc, h2, n_tiles * TM)
    y = _experts(tile_expert, n_active.reshape(1).astype(jnp.int32), xs, w1, w3, w2)
    return _combine(runs_flat, loc, x, route, y, out_rows)


def _tile_edges(seq_lens):
    first, last = [], []
    for n in seq_lens:
        t = n // TM
        first += [1] + [0] * (t - 1)
        last += [0] * (t - 1) + [1]
    return jnp.asarray(first, jnp.int32), jnp.asarray(last, jnp.int32)


def kernel(x_prompt, x_sample, norm1_g, w_in, b_gate, conv_w, conv_b, conv_ln_g, conv_ln_b,
           q_norm_g, k_norm_g, rpb, w_proj_a, w_proj_b, w_out, norm2_g, ffn_w1, ffn_w3, ffn_w2,
           router_w, router_b, moe_w1, moe_w3, moe_w2):
    depth = w_in.shape[0]
    seq_lens = [x_prompt.shape[1]] * x_prompt.shape[0] + [x_sample.shape[1]] * x_sample.shape[0]
    assert all(n % TM == 0 and n // GRID_W >= 2 * TILE_ROWS for n in seq_lens)
    first, last = _tile_edges(seq_lens)
    x_parts = [x_prompt.reshape(-1, D_MODEL), x_sample.reshape(-1, D_MODEL)]
    out_rows = [p.shape[0] for p in x_parts]

    head_id = jnp.arange(D_ATTN) // HEAD_DIM
    head_mean = jnp.where(head_id[:, None] == head_id[None, :], 1.0 / HEAD_DIM, 0.0).astype(BF16)
    row = lambda a: a.reshape(1, -1).astype(F32)

    for l in range(depth):
        final = out_rows if l == depth - 1 else None
        table = _bias_table(rpb[l])
        vglu, q, k, v, ga, gb = _in_proj(
            x_parts, row(norm1_g[l]), _to_bf16(w_in, l), row(b_gate[l]),
            row(jnp.tile(q_norm_g[l], N_HEADS)), row(jnp.tile(k_norm_g[l], N_HEADS)), head_mean)
        j = l // 2
        moe = l % 2 == 1
        if moe:
            rw = jnp.pad(router_w[j], ((0, 0), (0, LANES - N_EXPERTS))).astype(BF16)
            rb = jnp.pad(row(router_b[j]), ((0, 0), (0, LANES - N_EXPERTS)))
        else:
            rw = jnp.zeros((D_MODEL, LANES), BF16)
            rb = jnp.zeros((1, LANES), F32)
        cw = jnp.pad(conv_w[l], ((0, 1), (0, 0)))
        outs = _mixer(
            x_parts, vglu, q, k, v, ga, gb, first, last, cw, row(conv_b[l]), row(conv_ln_g[l]),
            row(conv_ln_b[l]), table, _to_bf16(w_proj_a, l), _to_bf16(w_proj_b, l),
            _to_bf16(w_out, l), row(norm2_g[l]), rw, rb, with_router=moe)
        if moe:
            x, h2, route, route_t, totals = outs
            res = _moe(h2, x, route, route_t, totals, _to_bf16(moe_w1[j]), _to_bf16(moe_w3[j]),
                       _to_bf16(moe_w2[j]), final)
        else:
            x, h2 = outs
            res = _ffn(h2, x, _to_bf16(ffn_w1, j), _to_bf16(ffn_w3, j), _to_bf16(ffn_w2, j),
                       final)
        x_parts = list(res)

    return (x_parts[0].reshape(x_prompt.shape), x_parts[1].reshape(x_sample.shape))
```

```python
import functools

import jax
import jax.numpy as jnp
from jax import lax
from jax.experimental import pallas as pl
from jax.experimental.pallas import tpu as pltpu

D_MODEL = 1024
GRID_W = 64
D_CONV = 512
CONV_W = 31
N_HEADS = 8
HEAD_DIM = 64
D_ATTN = N_HEADS * HEAD_DIM
MAX_WIN_H = 8
WIN_W = 16
D_FF = 2816
N_EXPERTS = 8
EPS = 1e-6

LANES = 128
SUBLANES = 8
BF16_ROWS = 16
TM = 512
TILE_ROWS = TM // GRID_W
KV_HALO = 256
CONV_HALO = 16
CONV_PAD = CONV_W // 2
N_RO = 2 * MAX_WIN_H - 1
N_CO = 2 * WIN_W - 1
HEAD_GROUP = 4
ROUTE_W = 8
ATT_UNROLL = 8
NEG_BIG = -1e30
LOG2E = 1.4426950408889634
FF_CHUNK = 256
VMEM_LIMIT = 56 * 1024 * 1024

F32 = jnp.float32
BF16 = jnp.bfloat16


def _sigmoid(x):
    return 1.0 / (1.0 + jnp.exp(-x))


def _round_up_rows(count):
    return jnp.floor((count + (SUBLANES - 1)) * (1.0 / SUBLANES)) * SUBLANES


def _part_specs(parts, width=D_MODEL):
    starts, specs, tile0 = [], [], 0
    for a in parts:
        n = a.shape[0] // TM
        specs.append(pl.BlockSpec(
            (TM, width), lambda i, *_, s=tile0, n=n: (jnp.clip(i - s, 0, n - 1), 0)))
        starts.append(tile0)
        tile0 += n
    return tuple(starts), specs, tile0


def _read_parts(refs, starts):
    i = pl.program_id(0)
    x = refs[0][...]
    for ref, s in zip(refs[1:], starts[1:]):
        x = jnp.where(i >= s, ref[...], x)
    return x


def _write_parts(refs, starts, value):
    i = pl.program_id(0)
    ends = starts[1:] + (None,)
    for ref, s, e in zip(refs, starts, ends):
        owns = i >= s if e is None else (i >= s) & (i < e)

        @pl.when(owns)
        def _():
            ref[...] = value


CAST_ROWS = 512
CAST_BLOCK_ELEMS = 3 * 1024 * 1024


def _cast_kernel(w_ref, o_ref):
    o_ref[...] = w_ref[...].astype(BF16)


def _to_bf16(w, take=None):
    lead = w.shape[:-2]
    k, n = w.shape[-2:]
    w3 = w.reshape((-1, k, n))
    n_mat = w3.shape[0] if take is None else 1
    first = 0 if take is None else take
    rows = k if k * n <= CAST_BLOCK_ELEMS else next(
        (r for r in (CAST_ROWS, CAST_ROWS // 2, CAST_ROWS // 4) if k % r == 0), k)
    out = pl.pallas_call(
        _cast_kernel,
        grid=(n_mat, k // rows),
        in_specs=[pl.BlockSpec((1, rows, n), lambda e, r: (first + e, r, 0))],
        out_specs=pl.BlockSpec((1, rows, n), lambda e, r: (e, r, 0)),
        out_shape=jax.ShapeDtypeStruct((n_mat, k, n), BF16),
        compiler_params=pltpu.CompilerParams(
            dimension_semantics=("arbitrary", "arbitrary"), vmem_limit_bytes=VMEM_LIMIT),
        name="to_bf16",
    )(w3)
    return out.reshape((k, n) if take is not None else lead + (k, n))


def _side_cast(w, n_steps):
    w2 = w.reshape(-1, w.shape[-1])
    rows = w2.shape[0] // n_steps
    if w2.shape[0] % n_steps or rows % BF16_ROWS:
        return None
    spec = pl.BlockSpec((rows, w2.shape[1]), lambda i, *_: (i, 0))
    return w2, spec, jax.ShapeDtypeStruct(w2.shape, BF16)


def _bias_table_kernel(rpb_ref, out_ref):
    ro = pl.program_id(0)
    shape = (GRID_W, HEAD_GROUP * GRID_W)
    kcol = lax.broadcasted_iota(jnp.int32, shape, 0)
    lane = lax.broadcasted_iota(jnp.int32, shape, 1)
    head_local = lane // GRID_W
    qcol = lane - head_local * GRID_W
    start_c = jnp.clip(qcol - WIN_W // 2, 0, GRID_W - WIN_W)
    valid = (kcol >= start_c) & (kcol < start_c + WIN_W)
    co = jnp.clip(kcol - qcol + (WIN_W - 1), 0, 2 * WIN_W - 2)
    for g in range(N_HEADS // HEAD_GROUP):
        acc = jnp.zeros(shape, F32)
        for d in range(N_CO):
            val = jnp.zeros(shape, F32)
            for hl in range(HEAD_GROUP):
                h = g * HEAD_GROUP + hl
                val = jnp.where(head_local == hl, rpb_ref[(h * N_RO + ro) * N_CO + d], val)
            acc = jnp.where(co == d, val, acc)
        out_ref[0, g] = jnp.where(valid, acc * LOG2E, NEG_BIG)


def _bias_table(rpb_l):
    n_groups = N_HEADS // HEAD_GROUP
    width = HEAD_GROUP * GRID_W
    return pl.pallas_call(
        _bias_table_kernel,
        grid=(N_RO,),
        in_specs=[pl.BlockSpec(memory_space=pltpu.SMEM)],
        out_specs=pl.BlockSpec((1, n_groups, GRID_W, width), lambda r: (r, 0, 0, 0)),
        out_shape=jax.ShapeDtypeStruct((N_RO, n_groups, GRID_W, width), F32),
        name="bias_table",
    )(rpb_l.reshape(-1))


def _in_proj_kernel(*refs, x_starts, side):
    n_parts = len(x_starts)
    g_ref, w_ref, bg_ref, qg_ref, kg_ref, hm_ref = refs[n_parts:n_parts + 6]
    outs = refs[n_parts + 6 + side:]
    vglu_ref, q_ref, k_ref, v_ref, ga_ref, gb_ref = outs[:6]
    if side:
        outs[6][...] = refs[n_parts + 6][...].astype(BF16)
    x = _read_parts(refs[:n_parts], x_starts)
    ms = jnp.mean(x * x, axis=-1, keepdims=True)
    h = (x * lax.rsqrt(ms + EPS) * g_ref[...]).astype(BF16)

    def proj(lo, hi):
        return jnp.dot(h, w_ref[:, lo:hi], preferred_element_type=F32)

    def head_norm(t, gain):
        msq = jnp.dot((t * t).astype(BF16), hm_ref[...], preferred_element_type=F32)
        return t * lax.rsqrt(msq + EPS) * gain

    a = proj(0, D_CONV)
    gate = proj(D_CONV, 2 * D_CONV)
    vglu_ref[...] = (a * _sigmoid(gate)).astype(BF16)
    o = 2 * D_CONV
    q_ref[...] = (head_norm(proj(o, o + D_ATTN), qg_ref[...])
                  * (HEAD_DIM ** -0.5 * LOG2E)).astype(BF16)
    k_ref[...] = head_norm(proj(o + D_ATTN, o + 2 * D_ATTN), kg_ref[...]).astype(BF16)
    v_ref[...] = proj(o + 2 * D_ATTN, o + 3 * D_ATTN).astype(BF16)
    o += 3 * D_ATTN
    bg = bg_ref[...]
    ga_ref[...] = _sigmoid(proj(o, o + D_MODEL) + bg[:, :D_MODEL]).astype(BF16)
    gb_ref[...] = _sigmoid(proj(o + D_MODEL, o + 2 * D_MODEL) + bg[:, D_MODEL:]).astype(BF16)


def _in_proj(x_parts, norm_g, w_in, b_gate, q_norm_g, k_norm_g, head_mean, side_w=None):
    x_starts, x_specs, n_tiles = _part_specs(x_parts)
    n_tok = n_tiles * TM
    d_in = w_in.shape[1]
    full = lambda shape: pl.BlockSpec(shape, lambda i: (0, 0))
    tile = lambda width: pl.BlockSpec((TM, width), lambda i: (i, 0))
    out = lambda width: jax.ShapeDtypeStruct((n_tok, width), BF16)
    plan = _side_cast(side_w, n_tiles) if side_w is not None else None
    side_args, side_in, side_out, side_shape = [], [], [], []
    if plan is not None:
        side_args, side_in, side_out, side_shape = [plan[0]], [plan[1]], [plan[1]], [plan[2]]
    res = pl.pallas_call(
        functools.partial(_in_proj_kernel, x_starts=x_starts, side=len(side_args)),
        grid=(n_tiles,),
        in_specs=x_specs + [full((1, D_MODEL)), full((D_MODEL, d_in)),
                            full((1, 2 * D_MODEL)), full((1, D_ATTN)), full((1, D_ATTN)),
                            full((D_ATTN, D_ATTN))] + side_in,
        out_specs=[tile(D_CONV), tile(D_ATTN), tile(D_ATTN), tile(D_ATTN),
                   tile(D_MODEL), tile(D_MODEL)] + side_out,
        out_shape=[out(D_CONV), out(D_ATTN), out(D_ATTN), out(D_ATTN),
                   out(D_MODEL), out(D_MODEL)] + side_shape,
        compiler_params=pltpu.CompilerParams(
            dimension_semantics=("arbitrary",), vmem_limit_bytes=VMEM_LIMIT),
        name="in_proj",
    )(*x_parts, norm_g, w_in, b_gate, q_norm_g, k_norm_g, head_mean, *side_args)
    side_bf16 = None
    if side_w is not None:
        side_bf16 = res[6].reshape(side_w.shape) if plan is not None else _to_bf16(side_w)
    return res[:6], side_bf16


def _mixer_kernel(first_ref, last_ref, *refs, x_starts, with_router):
    n_parts = len(x_starts)
    (vg_ref, vgp_ref, vgn_ref, q_ref,
     k_ref, kp_ref, kn_ref, v_ref, vp_ref, vn_ref,
     ga_ref, gb_ref, cw_ref, cb_ref, lg_ref, lb_ref, tab_ref,
     wa_ref, wb_ref, wo_ref, n2_ref, rw_ref, rb_ref) = refs[n_parts:n_parts + 23]
    rest = refs[n_parts + 23:]
    if with_router:
        xo_ref, h2_ref, route_ref, route_t_ref, tot_ref = rest[:5]
    else:
        xo_ref, h2_ref = rest[:2]
    ext_ref, sh_ref, conv_ref, kext_ref, vext_ref, att_ref = rest[-6:]
    i = pl.program_id(0)
    is_first = first_ref[i]
    is_last = last_ref[i]

    keep_prev = (1 - is_first).astype(F32)
    keep_next = (1 - is_last).astype(F32)
    ext_ref[0:CONV_HALO, :] = vgp_ref[...].astype(F32) * keep_prev
    ext_ref[CONV_HALO:CONV_HALO + TM, :] = vg_ref[...].astype(F32)
    ext_ref[CONV_HALO + TM:, :] = vgn_ref[...].astype(F32) * keep_next
    n_sh_rows = sh_ref.shape[1]
    row_chunk = 128
    for cb in range(D_CONV // LANES):
        cols = slice(cb * LANES, (cb + 1) * LANES)
        for b in range(SUBLANES):
            sh_ref[b] = ext_ref[b:b + n_sh_rows, cols]

        def conv_rows(rc, carry, cols=cols):
            base = pl.multiple_of(rc * row_chunk, row_chunk)
            acc = jnp.zeros((row_chunk, LANES), F32)
            for j in range(CONV_W):
                off = CONV_HALO - CONV_PAD + j
                a8, b = off // SUBLANES, off % SUBLANES
                acc = acc + (sh_ref[b, pl.ds(base + SUBLANES * a8, row_chunk), :]
                             * cw_ref[j:j + 1, cols])
            conv_ref[pl.ds(base, row_chunk), cols] = acc
            return carry

        lax.fori_loop(0, TM // row_chunk, conv_rows, 0)
    y = conv_ref[...] + cb_ref[...]
    mu = jnp.mean(y, axis=-1, keepdims=True)
    yc = y - mu
    var = jnp.mean(yc * yc, axis=-1, keepdims=True)
    yn = yc * lax.rsqrt(var + EPS) * lg_ref[...] + lb_ref[...]
    act = (yn * _sigmoid(yn)).astype(BF16)
    br_a = jnp.dot(act, wa_ref[...], preferred_element_type=F32)

    kext_ref[0:KV_HALO, :] = kp_ref[...]
    kext_ref[KV_HALO:KV_HALO + TM, :] = k_ref[...]
    kext_ref[KV_HALO + TM:, :] = kn_ref[...]
    vext_ref[0:KV_HALO, :] = vp_ref[...]
    vext_ref[KV_HALO:KV_HALO + TM, :] = v_ref[...]
    vext_ref[KV_HALO + TM:, :] = vn_ref[...]
    gw = HEAD_GROUP * HEAD_DIM
    lane_head = lax.broadcasted_iota(jnp.int32, (GRID_W, gw), 1) // HEAD_DIM
    win = MAX_WIN_H * GRID_W

    def row_body(j, carry):
        ws = j - MAX_WIN_H // 2
        ws = jnp.where(is_first == 1, jnp.maximum(ws, 0), ws)
        ws = jnp.where(is_last == 1, jnp.minimum(ws, 0), ws)
        ro0 = ws - j + (MAX_WIN_H - 1)
        kstart = pl.multiple_of((ws + KV_HALO // GRID_W) * GRID_W, GRID_W)
        qstart = pl.multiple_of(j * GRID_W, GRID_W)
        for g in range(N_HEADS // HEAD_GROUP):
            cols = slice(g * gw, (g + 1) * gw)
            q4 = q_ref[pl.ds(qstart, GRID_W), cols]
            qexp = jnp.concatenate(
                [jnp.where(lane_head == hl, q4, jnp.zeros_like(q4)) for hl in range(HEAD_GROUP)],
                axis=0)
            kslab = kext_ref[pl.ds(kstart, win), cols]
            st = lax.dot_general(kslab, qexp, (((1,), (1,)), ((), ())),
                                 preferred_element_type=F32)
            st = st + jnp.concatenate([tab_ref[ro0 + w, g] for w in range(MAX_WIN_H)], axis=0)
            m = jnp.max(st, axis=0, keepdims=True)
            pexp = jnp.exp2(st - m)
            inv = 1.0 / jnp.sum(pexp, axis=0, keepdims=True)
            pn = (pexp * inv).astype(BF16)
            vslab = vext_ref[pl.ds(kstart, win), cols]
            o4 = lax.dot_general(pn, vslab, (((0,), (0,)), ((), ())),
                                 preferred_element_type=F32)
            out = o4[0:GRID_W]
            for hl in range(1, HEAD_GROUP):
                out = jnp.where(lane_head == hl, o4[hl * GRID_W:(hl + 1) * GRID_W], out)
            att_ref[pl.ds(qstart, GRID_W), cols] = out
        return carry

    lax.fori_loop(0, TILE_ROWS, row_body, 0, unroll=ATT_UNROLL)
    br_b = jnp.dot(att_ref[...].astype(BF16), wb_ref[...], preferred_element_type=F32)

    merged = ga_ref[...].astype(F32) * br_a + gb_ref[...].astype(F32) * br_b
    x_new = (_read_parts(refs[:n_parts], x_starts)
             + jnp.dot(merged.astype(BF16), wo_ref[...], preferred_element_type=F32))
    xo_ref[...] = x_new
    ms = jnp.mean(x_new * x_new, axis=-1, keepdims=True)
    h2 = x_new * lax.rsqrt(ms + EPS) * n2_ref[...]
    h2_ref[...] = h2.astype(h2_ref.dtype)

    if with_router:
        logits = jnp.dot(h2.astype(BF16), rw_ref[...], preferred_element_type=F32) + rb_ref[...]
        elane = lax.broadcasted_iota(jnp.int32, logits.shape, 1).astype(F32)
        logits = jnp.where(elane < N_EXPERTS, logits, -jnp.inf)
        m1 = jnp.max(logits, axis=1, keepdims=True)
        i1 = jnp.min(jnp.where(logits == m1, elane, float(LANES)), axis=1, keepdims=True)
        others = jnp.where(elane == i1, -jnp.inf, logits)
        m2 = jnp.max(others, axis=1, keepdims=True)
        i2 = jnp.min(jnp.where(others == m2, elane, float(LANES)), axis=1, keepdims=True)
        e2 = jnp.exp(m2 - m1)
        w1 = 1.0 / (1.0 + e2)
        w2 = e2 / (1.0 + e2)
        route = jnp.where(elane == 0, i1, jnp.where(elane == 1, i2, jnp.where(
            elane == 2, w1, jnp.where(elane == 3, w2, 0.0))))
        route_ref[...] = route[:, :ROUTE_W]
        route_t_ref[0] = route.T[:ROUTE_W, :]
        onehot = ((elane == i1) | (elane == i2)).astype(F32)

        @pl.when(i == 0)
        def _():
            tot_ref[...] = jnp.zeros(tot_ref.shape, F32)

        tot_ref[...] += jnp.broadcast_to(
            _round_up_rows(jnp.sum(onehot, axis=0, keepdims=True)), tot_ref.shape)


def _mixer(x_parts, vglu, q, k, v, ga, gb, first, last, conv_w, conv_b, ln_g, ln_b, table,
           w_a, w_b, w_o, norm2_g, router_w, router_b, *, with_router):
    x_starts, x_specs, n_tiles = _part_specs(x_parts)
    n_tok = n_tiles * TM
    conv_blocks = TM // CONV_HALO
    kv_blocks = TM // KV_HALO
    full2 = lambda shape: pl.BlockSpec(shape, lambda i, f, l: (0, 0))
    tile = lambda width: pl.BlockSpec((TM, width), lambda i, f, l: (i, 0))
    in_specs = x_specs + [
        tile(D_CONV),
        pl.BlockSpec((CONV_HALO, D_CONV), lambda i, f, l: (i * conv_blocks - 1 + f[i], 0)),
        pl.BlockSpec((CONV_HALO, D_CONV), lambda i, f, l: ((i + 1) * conv_blocks - l[i], 0)),
        tile(D_ATTN),
        tile(D_ATTN),
        pl.BlockSpec((KV_HALO, D_ATTN), lambda i, f, l: (i * kv_blocks - 1 + f[i], 0)),
        pl.BlockSpec((KV_HALO, D_ATTN), lambda i, f, l: ((i + 1) * kv_blocks - l[i], 0)),
        tile(D_ATTN),
        pl.BlockSpec((KV_HALO, D_ATTN), lambda i, f, l: (i * kv_blocks - 1 + f[i], 0)),
        pl.BlockSpec((KV_HALO, D_ATTN), lambda i, f, l: ((i + 1) * kv_blocks - l[i], 0)),
        tile(D_MODEL),
        tile(D_MODEL),
        full2((CONV_W + 1, D_CONV)),
        full2((1, D_CONV)), full2((1, D_CONV)), full2((1, D_CONV)),
        pl.BlockSpec(table.shape, lambda i, f, l: (0, 0, 0, 0)),
        full2((D_CONV, D_MODEL)), full2((D_ATTN, D_MODEL)), full2((D_MODEL, D_MODEL)),
        full2((1, D_MODEL)),
        full2((D_MODEL, LANES)), full2((1, LANES)),
    ]
    out_specs = [tile(D_MODEL), tile(D_MODEL)]
    out_shape = [jax.ShapeDtypeStruct((n_tok, D_MODEL), F32),
                 jax.ShapeDtypeStruct((n_tok, D_MODEL), F32 if with_router else BF16)]
    if with_router:
        out_specs += [tile(ROUTE_W),
                      pl.BlockSpec((1, ROUTE_W, TM), lambda i, f, l: (i, 0, 0)),
                      pl.BlockSpec((8, LANES), lambda i, f, l: (0, 0))]
        out_shape += [jax.ShapeDtypeStruct((n_tok, ROUTE_W), F32),
                      jax.ShapeDtypeStruct((n_tok // TM, ROUTE_W, TM), F32),
                      jax.ShapeDtypeStruct((8, LANES), F32)]
    ext_rows = TM + 2 * CONV_HALO
    scratch = [
        pltpu.VMEM((ext_rows, D_CONV), F32),
        pltpu.VMEM((SUBLANES, ext_rows - SUBLANES, LANES), F32),
        pltpu.VMEM((TM, D_CONV), F32),
        pltpu.VMEM((TM + 2 * KV_HALO, D_ATTN), BF16),
        pltpu.VMEM((TM + 2 * KV_HALO, D_ATTN), BF16),
        pltpu.VMEM((TM, D_ATTN), F32),
    ]
    return pl.pallas_call(
        functools.partial(_mixer_kernel, x_starts=x_starts, with_router=with_router),
        grid_spec=pltpu.PrefetchScalarGridSpec(
            num_scalar_prefetch=2, grid=(n_tiles,),
            in_specs=in_specs, out_specs=out_specs, scratch_shapes=scratch),
        out_shape=out_shape,
        compiler_params=pltpu.CompilerParams(
            dimension_semantics=("arbitrary",), vmem_limit_bytes=VMEM_LIMIT),
        name="mixer",
    )(first, last, *x_parts, vglu, vglu, vglu, q, k, k, k, v, v, v, ga, gb,
      conv_w, conv_b, ln_g, ln_b, table, w_a, w_b, w_o, norm2_g, router_w, router_b)


def _ffn_kernel(h_ref, x_ref, w1_ref, w3_ref, w2_ref, *rest, o_starts, side):
    o_refs = rest
    if side:
        rest[-1][...] = rest[0][...].astype(BF16)
        o_refs = rest[1:-1]
    h = h_ref[...]
    acc = x_ref[...]
    for c in range(0, D_FF, FF_CHUNK):
        a = jnp.dot(h, w1_ref[:, c:c + FF_CHUNK], preferred_element_type=F32)
        b = jnp.dot(h, w3_ref[:, c:c + FF_CHUNK], preferred_element_type=F32)
        act = (a * _sigmoid(a) * b).astype(BF16)
        acc = acc + jnp.dot(act, w2_ref[c:c + FF_CHUNK, :], preferred_element_type=F32)
    _write_parts(o_refs, o_starts, acc)


def _out_parts(n_tok, out_rows):
    shapes = [jax.ShapeDtypeStruct((r, D_MODEL), F32) for r in (out_rows or [n_tok])]
    starts, specs, _ = _part_specs(shapes)
    return shapes, starts, specs


def _ffn(h2, x, w1, w3, w2, out_rows=None, side_w=None):
    n_tok = x.shape[0]
    tile = lambda: pl.BlockSpec((TM, D_MODEL), lambda i: (i, 0))
    single = dict(pipeline_mode=pl.Buffered(1))
    o_shapes, o_starts, o_specs = _out_parts(n_tok, out_rows)
    plan = _side_cast(side_w, n_tok // TM) if side_w is not None else None
    side_args, side_spec, side_shape = [], [], []
    if plan is not None:
        side_args, side_spec, side_shape = [plan[0]], [plan[1]], [plan[2]]
    res = pl.pallas_call(
        functools.partial(_ffn_kernel, o_starts=o_starts, side=len(side_args)),
        grid=(n_tok // TM,),
        in_specs=[tile(), tile(),
                  pl.BlockSpec((D_MODEL, D_FF), lambda i: (0, 0), **single),
                  pl.BlockSpec((D_MODEL, D_FF), lambda i: (0, 0), **single),
                  pl.BlockSpec((D_FF, D_MODEL), lambda i: (0, 0), **single)] + side_spec,
        out_specs=o_specs + side_spec,
        out_shape=o_shapes + side_shape,
        compiler_params=pltpu.CompilerParams(
            dimension_semantics=("arbitrary",), vmem_limit_bytes=VMEM_LIMIT),
        name="ffn",
    )(h2, x, w1, w3, w2, *side_args)
    side_bf16 = None
    if side_w is not None:
        side_bf16 = res[-1].reshape(side_w.shape) if plan is not None else _to_bf16(side_w)
    return res[:len(o_shapes)], side_bf16


RUN_FIELDS = 3
PERM_ROWS = 2 * TM + N_EXPERTS * SUBLANES


def _slots_kernel(rt_ref, start_ref, loc_ref, runs_ref, run_ref):
    i = pl.program_id(0)

    @pl.when(i == 0)
    def _():
        run_ref[...] = jnp.zeros(run_ref.shape, F32)

    rt = rt_ref[0]
    eidx = lax.broadcasted_iota(jnp.int32, (N_EXPERTS, TM), 0).astype(F32)
    e1 = eidx == rt[0:1, :]
    e2 = eidx == rt[1:2, :]
    chosen = (e1 | e2).astype(F32)
    before = lax.broadcasted_iota(jnp.int32, (TM, TM), 0) < lax.broadcasted_iota(
        jnp.int32, (TM, TM), 1)
    rank = jnp.dot(chosen.astype(BF16), before.astype(BF16), preferred_element_type=F32)
    rows = _round_up_rows(jnp.sum(chosen, axis=1, keepdims=True))
    rows_b = jnp.broadcast_to(rows, (N_EXPERTS, LANES))
    expert = lax.broadcasted_iota(jnp.int32, (N_EXPERTS, LANES), 0)
    incl = rows_b
    for step in (1, 2, 4):
        incl = incl + jnp.where(expert >= step, pltpu.roll(incl, step, axis=0), 0.0)
    local = incl - rows_b
    pos = rank + local[:, 0:1]
    l1 = jnp.sum(jnp.where(e1, pos, 0.0), axis=0, keepdims=True)
    l2 = jnp.sum(jnp.where(e2, pos, 0.0), axis=0, keepdims=True)
    loc_ref[0] = jnp.concatenate([l1, l2], axis=0).astype(jnp.int32)
    lane = lax.broadcasted_iota(jnp.int32, (N_EXPERTS, LANES), 1)
    dest = start_ref[...] + run_ref[...]
    runs_ref[0] = jnp.where(lane == 0, dest, jnp.where(lane == 1, rows_b, local)).astype(jnp.int32)
    run_ref[...] += rows_b


def _slots(route_t, start):
    n_tiles = route_t.shape[0]
    return pl.pallas_call(
        _slots_kernel,
        grid=(n_tiles,),
        in_specs=[pl.BlockSpec((1, ROUTE_W, TM), lambda i: (i, 0, 0)),
                  pl.BlockSpec((N_EXPERTS, LANES), lambda i: (0, 0))],
        out_specs=[pl.BlockSpec((1, 2, TM), lambda i: (i, 0, 0)),
                   pl.BlockSpec((1, N_EXPERTS, LANES), lambda i: (i, 0, 0))],
        out_shape=[jax.ShapeDtypeStruct((n_tiles, 2, TM), jnp.int32),
                   jax.ShapeDtypeStruct((n_tiles, N_EXPERTS, LANES), jnp.int32)],
        scratch_shapes=[pltpu.VMEM((N_EXPERTS, LANES), F32)],
        compiler_params=pltpu.CompilerParams(dimension_semantics=("arbitrary",)),
        name="moe_slots",
    )(route_t, start)


def _run_copies(runs_ref, tile, make_copy, wait):
    for e in range(N_EXPERTS):
        base = (tile * N_EXPERTS + e) * RUN_FIELDS
        sorted_row, n, local_row = runs_ref[base], runs_ref[base + 1], runs_ref[base + 2]
        for b in range(SUBLANES.bit_length() - 1, TM.bit_length()):
            size = 1 << b

            @pl.when(((n >> b) & 1) == 1)
            def _():
                done = n & (size - 1)
                copy = make_copy(pl.multiple_of(sorted_row + done, SUBLANES),
                                 pl.multiple_of(local_row + done, SUBLANES), size)
                copy.wait() if wait else copy.start()


def _perm_matrix(loc_row):
    rows = lax.broadcasted_iota(jnp.int32, (PERM_ROWS, TM), 0)
    return rows == loc_row


ZERO_ROWS = 64


def _dispatch_kernel(ts_ref, tl_ref, runs_ref, loc_ref, h_ref, xs_ref,
                     zero_ref, grouped_ref, sem, zsem):
    i = pl.program_id(0)

    def fill_padding(wait):
        def run(copy):
            copy.wait() if wait else copy.start()

        for e in range(N_EXPERTS):
            base = ts_ref[e]
            n = tl_ref[e]
            full = n // ZERO_ROWS

            def chunk(c, carry):
                off = pl.multiple_of(base + c * ZERO_ROWS, SUBLANES)
                run(pltpu.make_async_copy(zero_ref, xs_ref.at[pl.ds(off, ZERO_ROWS)], zsem))
                return carry

            lax.fori_loop(0, full, chunk, 0)
            rem_base = base + full * ZERO_ROWS
            for b in range(SUBLANES.bit_length() - 1, ZERO_ROWS.bit_length() - 1):
                size = 1 << b

                @pl.when(((n >> b) & 1) == 1)
                def _():
                    off = pl.multiple_of(rem_base + (n & (size - 1)), SUBLANES)
                    run(pltpu.make_async_copy(
                        zero_ref.at[pl.ds(0, size)], xs_ref.at[pl.ds(off, size)], zsem))

    @pl.when(i == 0)
    def _():
        zero_ref[...] = jnp.zeros(zero_ref.shape, F32)
        fill_padding(wait=False)

    slot = lax.rem(i, 2)
    loc = loc_ref[0]
    perm = (_perm_matrix(loc[0:1, :]) | _perm_matrix(loc[1:2, :])).astype(BF16)
    grouped_ref[slot] = jnp.dot(perm, h_ref[...].astype(BF16), preferred_element_type=F32)

    def run_copy(buf):
        def make(sorted_row, local_row, size):
            return pltpu.make_async_copy(grouped_ref.at[buf, pl.ds(local_row, size)],
                                         xs_ref.at[pl.ds(sorted_row, size)], sem.at[buf])
        return make

    _run_copies(runs_ref, i, run_copy(slot), wait=False)

    @pl.when(i > 0)
    def _():
        _run_copies(runs_ref, i - 1, run_copy(1 - slot), wait=True)

    @pl.when(i == pl.num_programs(0) - 1)
    def _():
        _run_copies(runs_ref, i, run_copy(slot), wait=True)

    @pl.when(i == 0)
    def _():
        fill_padding(wait=True)


def _dispatch(tail_start, tail_len, runs_flat, loc, h2, n_rows):
    n_tok = h2.shape[0]
    return pl.pallas_call(
        _dispatch_kernel,
        grid_spec=pltpu.PrefetchScalarGridSpec(
            num_scalar_prefetch=3, grid=(n_tok // TM,),
            in_specs=[pl.BlockSpec((1, 2, TM), lambda i, *_: (i, 0, 0)),
                      pl.BlockSpec((TM, D_MODEL), lambda i, *_: (i, 0))],
            out_specs=pl.BlockSpec(memory_space=pl.ANY),
            scratch_shapes=[pltpu.VMEM((ZERO_ROWS, D_MODEL), F32),
                            pltpu.VMEM((2, PERM_ROWS, D_MODEL), F32),
                            pltpu.SemaphoreType.DMA((2,)), pltpu.SemaphoreType.DMA]),
        out_shape=jax.ShapeDtypeStruct((n_rows, D_MODEL), F32),
        compiler_params=pltpu.CompilerParams(
            dimension_semantics=("arbitrary",), vmem_limit_bytes=VMEM_LIMIT),
        name="moe_dispatch",
    )(tail_start, tail_len, runs_flat, loc, h2)


def _expert_kernel(te_ref, na_ref, x_ref, w1_ref, w3_ref, w2_ref, y_ref):
    j = pl.program_id(0)

    @pl.when(j < na_ref[0])
    def _():
        h = x_ref[...].astype(BF16)
        acc = jnp.zeros((TM, D_MODEL), F32)
        for c in range(0, D_FF, FF_CHUNK):
            a = jnp.dot(h, w1_ref[0, :, c:c + FF_CHUNK], preferred_element_type=F32)
            b = jnp.dot(h, w3_ref[0, :, c:c + FF_CHUNK], preferred_element_type=F32)
            act = (a * _sigmoid(a) * b).astype(BF16)
            acc = acc + jnp.dot(act, w2_ref[0, c:c + FF_CHUNK, :], preferred_element_type=F32)
        y_ref[...] = acc

    @pl.when(j >= na_ref[0])
    def _():
        y_ref[...] = jnp.zeros(y_ref.shape, F32)


def _experts(tile_expert, n_active, xs, w1, w3, w2):
    n_tiles = xs.shape[0] // TM
    weights = lambda shape: pl.BlockSpec(shape, lambda j, te, na: (te[j], 0, 0))
    return pl.pallas_call(
        _expert_kernel,
        grid_spec=pltpu.PrefetchScalarGridSpec(
            num_scalar_prefetch=2, grid=(n_tiles,),
            in_specs=[pl.BlockSpec((TM, D_MODEL),
                                   lambda j, te, na: (jnp.minimum(j, na[0] - 1), 0)),
                      weights((1, D_MODEL, D_FF)), weights((1, D_MODEL, D_FF)),
                      weights((1, D_FF, D_MODEL))],
            out_specs=pl.BlockSpec((TM, D_MODEL), lambda j, te, na: (j, 0))),
        out_shape=jax.ShapeDtypeStruct(xs.shape, F32),
        compiler_params=pltpu.CompilerParams(
            dimension_semantics=("arbitrary",), vmem_limit_bytes=VMEM_LIMIT),
        name="moe_experts",
    )(tile_expert, n_active, xs, w1, w3, w2)


def _combine_kernel(runs_ref, loc_ref, x_ref, route_ref, y_ref, *refs, o_starts):
    stage_ref, sem = refs[-2:]
    i = pl.program_id(0)
    slot = lax.rem(i, 2)

    def run_copy(buf):
        def make(sorted_row, local_row, size):
            return pltpu.make_async_copy(y_ref.at[pl.ds(sorted_row, size)],
                                         stage_ref.at[buf, pl.ds(local_row, size)], sem.at[buf])
        return make

    @pl.when(i == 0)
    def _():
        stage_ref[...] = jnp.zeros(stage_ref.shape, F32)
        _run_copies(runs_ref, i, run_copy(slot), wait=False)

    @pl.when(i + 1 < pl.num_programs(0))
    def _():
        _run_copies(runs_ref, i + 1, run_copy(1 - slot), wait=False)

    _run_copies(runs_ref, i, run_copy(slot), wait=True)
    grouped = stage_ref[slot].astype(BF16)
    loc = loc_ref[0]
    route = route_ref[...]
    out = x_ref[...]
    for c in range(2):
        picked = lax.dot_general(_perm_matrix(loc[c:c + 1, :]).astype(BF16), grouped,
                                 (((0,), (0,)), ((), ())), preferred_element_type=F32)
        out = out + route[:, 2 + c:3 + c] * picked
    _write_parts(refs[:-2], o_starts, out)


def _combine(runs_flat, loc, x, route, y, out_rows=None):
    n_tok = x.shape[0]
    o_shapes, o_starts, o_specs = _out_parts(n_tok, out_rows)
    return pl.pallas_call(
        functools.partial(_combine_kernel, o_starts=o_starts),
        grid_spec=pltpu.PrefetchScalarGridSpec(
            num_scalar_prefetch=1, grid=(n_tok // TM,),
            in_specs=[pl.BlockSpec((1, 2, TM), lambda i, *_: (i, 0, 0)),
                      pl.BlockSpec((TM, D_MODEL), lambda i, *_: (i, 0)),
                      pl.BlockSpec((TM, ROUTE_W), lambda i, *_: (i, 0)),
                      pl.BlockSpec(memory_space=pl.ANY)],
            out_specs=o_specs,
            scratch_shapes=[pltpu.VMEM((2, PERM_ROWS, D_MODEL), F32),
                            pltpu.SemaphoreType.DMA((2,))]),
        out_shape=o_shapes,
        compiler_params=pltpu.CompilerParams(
            dimension_semantics=("arbitrary",), vmem_limit_bytes=VMEM_LIMIT),
        name="moe_combine",
    )(runs_flat, loc, x, route, y)


def _moe(h2, x, route, route_t, totals, w1, w3, w2, out_rows=None):
    n_tok = x.shape[0]
    tot = totals[0, :N_EXPERTS].astype(jnp.int32)
    padded = (tot + TM - 1) // TM * TM
    end = jnp.cumsum(padded)
    start = end - padded
    max_rows = 2 * n_tok + (n_tok // TM) * N_EXPERTS * (SUBLANES - 1) + N_EXPERTS * (TM - 1)
    n_tiles = max_rows // TM
    n_active = end[-1] // TM
    tile = jnp.minimum(jnp.arange(n_tiles, dtype=jnp.int32), n_active - 1)
    tile_expert = jnp.sum(tile[:, None] * TM >= end[None, :], axis=1).astype(jnp.int32)
    start_b = jnp.broadcast_to(start.astype(F32)[:, None], (N_EXPERTS, LANES))
    tail_start = (start + tot).astype(jnp.int32)
    tail_len = (end.at[-1].set(n_tiles * TM) - tail_start).astype(jnp.int32)

    loc, runs = _slots(route_t, start_b)
    runs_flat = runs[:, :, :RUN_FIELDS].reshape(-1)
    xs = _dispatch(tail_start, tail_len, runs_flat, loc, h2, n_tiles * TM)
    y = _experts(tile_expert, n_active.reshape(1).astype(jnp.int32), xs, w1, w3, w2)
    return _combine(runs_flat, loc, x, route, y, out_rows)


def _tile_edges(seq_lens):
    first, last = [], []
    for n in seq_lens:
        t = n // TM
        first += [1] + [0] * (t - 1)
        last += [0] * (t - 1) + [1]
    return jnp.asarray(first, jnp.int32), jnp.asarray(last, jnp.int32)


def kernel(x_prompt, x_sample, norm1_g, w_in, b_gate, conv_w, conv_b, conv_ln_g, conv_ln_b,
           q_norm_g, k_norm_g, rpb, w_proj_a, w_proj_b, w_out, norm2_g, ffn_w1, ffn_w3, ffn_w2,
           router_w, router_b, moe_w1, moe_w3, moe_w2):
    depth = w_in.shape[0]
    seq_lens = [x_prompt.shape[1]] * x_prompt.shape[0] + [x_sample.shape[1]] * x_sample.shape[0]
    assert all(n % TM == 0 and n // GRID_W >= 2 * TILE_ROWS for n in seq_lens)
    first, last = _tile_edges(seq_lens)
    x_parts = [x_prompt.reshape(-1, D_MODEL), x_sample.reshape(-1, D_MODEL)]
    out_rows = [p.shape[0] for p in x_parts]

    head_id = jnp.arange(D_ATTN) // HEAD_DIM
    head_mean = jnp.where(head_id[:, None] == head_id[None, :], 1.0 / HEAD_DIM, 0.0).astype(BF16)
    row = lambda a: a.reshape(1, -1).astype(F32)

    moe_bf16 = {}
    for l in range(depth):
        final = out_rows if l == depth - 1 else None
        j = l // 2
        moe = l % 2 == 1
        feeds_moe = not moe and l + 1 < depth
        table = _bias_table(rpb[l])
        side_w = moe_w2[j] if moe else (moe_w1[j] if feeds_moe else None)
        (vglu, q, k, v, ga, gb), side = _in_proj(
            x_parts, row(norm1_g[l]), _to_bf16(w_in, l), row(b_gate[l]),
            row(jnp.tile(q_norm_g[l], N_HEADS)), row(jnp.tile(k_norm_g[l], N_HEADS)), head_mean,
            side_w=side_w)
        if side is not None:
            moe_bf16["w2" if moe else "w1", j] = side
        if moe:
            rw = jnp.pad(router_w[j], ((0, 0), (0, LANES - N_EXPERTS))).astype(BF16)
            rb = jnp.pad(row(router_b[j]), ((0, 0), (0, LANES - N_EXPERTS)))
        else:
            rw = jnp.zeros((D_MODEL, LANES), BF16)
            rb = jnp.zeros((1, LANES), F32)
        cw = jnp.pad(conv_w[l], ((0, 1), (0, 0)))
        outs = _mixer(
            x_parts, vglu, q, k, v, ga, gb, first, last, cw, row(conv_b[l]), row(conv_ln_g[l]),
            row(conv_ln_b[l]), table, _to_bf16(w_proj_a, l), _to_bf16(w_proj_b, l),
            _to_bf16(w_out, l), row(norm2_g[l]), rw, rb, with_router=moe)
        if moe:
            x, h2, route, route_t, totals = outs
            w1 = moe_bf16.get(("w1", j))
            w3 = moe_bf16.get(("w3", j))
            res = _moe(h2, x, route, route_t, totals,
                       _to_bf16(moe_w1[j]) if w1 is None else w1,
                       _to_bf16(moe_w3[j]) if w3 is None else w3,
                       moe_bf16["w2", j], final)
        else:
            x, h2 = outs
            res, side = _ffn(h2, x, _to_bf16(ffn_w1, j), _to_bf16(ffn_w3, j),
                             _to_bf16(ffn_w2, j), final,
                             side_w=moe_w3[j] if feeds_moe else None)
            if side is not None:
                moe_bf16["w3", j] = side
        x_parts = list(res)

    return (x_parts[0].reshape(x_prompt.shape), x_parts[1].reshape(x_sample.shape))
```

```python
import functools

import jax
import jax.numpy as jnp
from jax import lax
from jax.experimental import pallas as pl
from jax.experimental.pallas import tpu as pltpu

D_MODEL = 1024
GRID_W = 64
D_CONV = 512
CONV_W = 31
N_HEADS = 8
HEAD_DIM = 64
D_ATTN = N_HEADS * HEAD_DIM
MAX_WIN_H = 8
WIN_W = 16
D_FF = 2816
N_EXPERTS = 8
EPS = 1e-6

LANES = 128
SUBLANES = 8
BF16_ROWS = 16
TM = 512
TILE_ROWS = TM // GRID_W
KV_HALO = 256
CONV_HALO = 16
CONV_PAD = CONV_W // 2
N_RO = 2 * MAX_WIN_H - 1
N_CO = 2 * WIN_W - 1
HEAD_GROUP = 4
ROUTE_W = 8
ATT_UNROLL = 8
NEG_BIG = -1e30
LOG2E = 1.4426950408889634
FF_CHUNK = 256
VMEM_LIMIT = 56 * 1024 * 1024

F32 = jnp.float32
BF16 = jnp.bfloat16


def _sigmoid(x):
    return 1.0 / (1.0 + jnp.exp(-x))


def _round_up_rows(count):
    return jnp.floor((count + (SUBLANES - 1)) * (1.0 / SUBLANES)) * SUBLANES


def _part_specs(parts, width=D_MODEL):
    starts, specs, tile0 = [], [], 0
    for a in parts:
        n = a.shape[0] // TM
        specs.append(pl.BlockSpec(
            (TM, width), lambda i, *_, s=tile0, n=n: (jnp.clip(i - s, 0, n - 1), 0)))
        starts.append(tile0)
        tile0 += n
    return tuple(starts), specs, tile0


def _read_parts(refs, starts):
    i = pl.program_id(0)
    x = refs[0][...]
    for ref, s in zip(refs[1:], starts[1:]):
        x = jnp.where(i >= s, ref[...], x)
    return x


def _write_parts(refs, starts, value):
    i = pl.program_id(0)
    ends = starts[1:] + (None,)
    for ref, s, e in zip(refs, starts, ends):
        owns = i >= s if e is None else (i >= s) & (i < e)

        @pl.when(owns)
        def _():
            ref[...] = value


CAST_ROWS = 512
CAST_BLOCK_ELEMS = 3 * 1024 * 1024


def _cast_kernel(w_ref, o_ref):
    o_ref[...] = w_ref[...].astype(BF16)


def _to_bf16(w, take=None):
    lead = w.shape[:-2]
    k, n = w.shape[-2:]
    w3 = w.reshape((-1, k, n))
    n_mat = w3.shape[0] if take is None else 1
    first = 0 if take is None else take
    rows = k if k * n <= CAST_BLOCK_ELEMS else next(
        (r for r in (CAST_ROWS, CAST_ROWS // 2, CAST_ROWS // 4) if k % r == 0), k)
    out = pl.pallas_call(
        _cast_kernel,
        grid=(n_mat, k // rows),
        in_specs=[pl.BlockSpec((1, rows, n), lambda e, r: (first + e, r, 0))],
        out_specs=pl.BlockSpec((1, rows, n), lambda e, r: (e, r, 0)),
        out_shape=jax.ShapeDtypeStruct((n_mat, k, n), BF16),
        compiler_params=pltpu.CompilerParams(
            dimension_semantics=("arbitrary", "arbitrary"), vmem_limit_bytes=VMEM_LIMIT),
        name="to_bf16",
    )(w3)
    return out.reshape((k, n) if take is not None else lead + (k, n))


def _side_cast(w, n_steps):
    w2 = w.reshape(-1, w.shape[-1])
    rows = w2.shape[0] // n_steps
    if w2.shape[0] % n_steps or rows % BF16_ROWS:
        return None
    spec = pl.BlockSpec((rows, w2.shape[1]), lambda i, *_: (i, 0))
    return w2, spec, jax.ShapeDtypeStruct(w2.shape, BF16)


def _bias_table_kernel(rpb_ref, out_ref):
    ro = pl.program_id(0)
    shape = (GRID_W, HEAD_GROUP * GRID_W)
    kcol = lax.broadcasted_iota(jnp.int32, shape, 0)
    lane = lax.broadcasted_iota(jnp.int32, shape, 1)
    head_local = lane // GRID_W
    qcol = lane - head_local * GRID_W
    start_c = jnp.clip(qcol - WIN_W // 2, 0, GRID_W - WIN_W)
    valid = (kcol >= start_c) & (kcol < start_c + WIN_W)
    co = jnp.clip(kcol - qcol + (WIN_W - 1), 0, 2 * WIN_W - 2)
    for g in range(N_HEADS // HEAD_GROUP):
        acc = jnp.zeros(shape, F32)
        for d in range(N_CO):
            val = jnp.zeros(shape, F32)
            for hl in range(HEAD_GROUP):
                h = g * HEAD_GROUP + hl
                val = jnp.where(head_local == hl, rpb_ref[(h * N_RO + ro) * N_CO + d], val)
            acc = jnp.where(co == d, val, acc)
        out_ref[0, g] = jnp.where(valid, acc * LOG2E, NEG_BIG)


def _bias_table(rpb_l):
    n_groups = N_HEADS // HEAD_GROUP
    width = HEAD_GROUP * GRID_W
    return pl.pallas_call(
        _bias_table_kernel,
        grid=(N_RO,),
        in_specs=[pl.BlockSpec(memory_space=pltpu.SMEM)],
        out_specs=pl.BlockSpec((1, n_groups, GRID_W, width), lambda r: (r, 0, 0, 0)),
        out_shape=jax.ShapeDtypeStruct((N_RO, n_groups, GRID_W, width), F32),
        name="bias_table",
    )(rpb_l.reshape(-1))


def _in_proj_kernel(*refs, x_starts, side):
    n_parts = len(x_starts)
    g_ref, w_ref, bg_ref, qg_ref, kg_ref, hm_ref = refs[n_parts:n_parts + 6]
    outs = refs[n_parts + 6 + side:]
    vglu_ref, q_ref, k_ref, v_ref, ga_ref, gb_ref = outs[:6]
    if side:
        outs[6][...] = refs[n_parts + 6][...].astype(BF16)
    x = _read_parts(refs[:n_parts], x_starts)
    ms = jnp.mean(x * x, axis=-1, keepdims=True)
    h = (x * lax.rsqrt(ms + EPS) * g_ref[...]).astype(BF16)

    def proj(lo, hi):
        return jnp.dot(h, w_ref[:, lo:hi], preferred_element_type=F32)

    def head_norm(t, gain):
        msq = jnp.dot((t * t).astype(BF16), hm_ref[...], preferred_element_type=F32)
        return t * lax.rsqrt(msq + EPS) * gain

    a = proj(0, D_CONV)
    gate = proj(D_CONV, 2 * D_CONV)
    vglu_ref[...] = (a * _sigmoid(gate)).astype(BF16)
    o = 2 * D_CONV
    q_ref[...] = (head_norm(proj(o, o + D_ATTN), qg_ref[...])
                  * (HEAD_DIM ** -0.5 * LOG2E)).astype(BF16)
    k_ref[...] = head_norm(proj(o + D_ATTN, o + 2 * D_ATTN), kg_ref[...]).astype(BF16)
    v_ref[...] = proj(o + 2 * D_ATTN, o + 3 * D_ATTN).astype(BF16)
    o += 3 * D_ATTN
    bg = bg_ref[...]
    ga_ref[...] = _sigmoid(proj(o, o + D_MODEL) + bg[:, :D_MODEL]).astype(BF16)
    gb_ref[...] = _sigmoid(proj(o + D_MODEL, o + 2 * D_MODEL) + bg[:, D_MODEL:]).astype(BF16)


def _in_proj(x_parts, norm_g, w_in, b_gate, q_norm_g, k_norm_g, head_mean, side_w=None):
    x_starts, x_specs, n_tiles = _part_specs(x_parts)
    n_tok = n_tiles * TM
    d_in = w_in.shape[1]
    full = lambda shape: pl.BlockSpec(shape, lambda i: (0, 0))
    tile = lambda width: pl.BlockSpec((TM, width), lambda i: (i, 0))
    out = lambda width: jax.ShapeDtypeStruct((n_tok, width), BF16)
    plan = _side_cast(side_w, n_tiles) if side_w is not None else None
    side_args, side_in, side_out, side_shape = [], [], [], []
    if plan is not None:
        side_args, side_in, side_out, side_shape = [plan[0]], [plan[1]], [plan[1]], [plan[2]]
    res = pl.pallas_call(
        functools.partial(_in_proj_kernel, x_starts=x_starts, side=len(side_args)),
        grid=(n_tiles,),
        in_specs=x_specs + [full((1, D_MODEL)), full((D_MODEL, d_in)),
                            full((1, 2 * D_MODEL)), full((1, D_ATTN)), full((1, D_ATTN)),
                            full((D_ATTN, D_ATTN))] + side_in,
        out_specs=[tile(D_CONV), tile(D_ATTN), tile(D_ATTN), tile(D_ATTN),
                   tile(D_MODEL), tile(D_MODEL)] + side_out,
        out_shape=[out(D_CONV), out(D_ATTN), out(D_ATTN), out(D_ATTN),
                   out(D_MODEL), out(D_MODEL)] + side_shape,
        compiler_params=pltpu.CompilerParams(
            dimension_semantics=("arbitrary",), vmem_limit_bytes=VMEM_LIMIT),
        name="in_proj",
    )(*x_parts, norm_g, w_in, b_gate, q_norm_g, k_norm_g, head_mean, *side_args)
    side_bf16 = None
    if side_w is not None:
        side_bf16 = res[6].reshape(side_w.shape) if plan is not None else _to_bf16(side_w)
    return res[:6], side_bf16


def _mixer_kernel(first_ref, last_ref, *refs, x_starts, with_router):
    n_parts = len(x_starts)
    (vg_ref, vgp_ref, vgn_ref, q_ref,
     k_ref, kp_ref, kn_ref, v_ref, vp_ref, vn_ref,
     ga_ref, gb_ref, cw_ref, cb_ref, lg_ref, lb_ref, tab_ref,
     wa_ref, wb_ref, wo_ref, n2_ref, rw_ref, rb_ref) = refs[n_parts:n_parts + 23]
    rest = refs[n_parts + 23:]
    if with_router:
        xo_ref, h2_ref, route_ref, route_t_ref, tot_ref = rest[:5]
    else:
        xo_ref, h2_ref = rest[:2]
    ext_ref, sh_ref, conv_ref, kext_ref, vext_ref, att_ref = rest[-6:]
    i = pl.program_id(0)
    is_first = first_ref[i]
    is_last = last_ref[i]

    keep_prev = (1 - is_first).astype(F32)
    keep_next = (1 - is_last).astype(F32)
    ext_ref[0:CONV_HALO, :] = vgp_ref[...].astype(F32) * keep_prev
    ext_ref[CONV_HALO:CONV_HALO + TM, :] = vg_ref[...].astype(F32)
    ext_ref[CONV_HALO + TM:, :] = vgn_ref[...].astype(F32) * keep_next
    n_sh_rows = sh_ref.shape[1]
    row_chunk = 128
    for cb in range(D_CONV // LANES):
        cols = slice(cb * LANES, (cb + 1) * LANES)
        for b in range(SUBLANES):
            sh_ref[b] = ext_ref[b:b + n_sh_rows, cols]

        def conv_rows(rc, carry, cols=cols):
            base = pl.multiple_of(rc * row_chunk, row_chunk)
            acc = jnp.zeros((row_chunk, LANES), F32)
            for j in range(CONV_W):
                off = CONV_HALO - CONV_PAD + j
                a8, b = off // SUBLANES, off % SUBLANES
                acc = acc + (sh_ref[b, pl.ds(base + SUBLANES * a8, row_chunk), :]
                             * cw_ref[j:j + 1, cols])
            conv_ref[pl.ds(base, row_chunk), cols] = acc
            return carry

        lax.fori_loop(0, TM // row_chunk, conv_rows, 0)
    y = conv_ref[...] + cb_ref[...]
    mu = jnp.mean(y, axis=-1, keepdims=True)
    yc = y - mu
    var = jnp.mean(yc * yc, axis=-1, keepdims=True)
    yn = yc * lax.rsqrt(var + EPS) * lg_ref[...] + lb_ref[...]
    act = (yn * _sigmoid(yn)).astype(BF16)
    br_a = jnp.dot(act, wa_ref[...], preferred_element_type=F32)

    kext_ref[0:KV_HALO, :] = kp_ref[...]
    kext_ref[KV_HALO:KV_HALO + TM, :] = k_ref[...]
    kext_ref[KV_HALO + TM:, :] = kn_ref[...]
    vext_ref[0:KV_HALO, :] = vp_ref[...]
    vext_ref[KV_HALO:KV_HALO + TM, :] = v_ref[...]
    vext_ref[KV_HALO + TM:, :] = vn_ref[...]
    gw = HEAD_GROUP * HEAD_DIM
    lane_head = lax.broadcasted_iota(jnp.int32, (GRID_W, gw), 1) // HEAD_DIM
    win = MAX_WIN_H * GRID_W

    def row_body(j, carry):
        ws = j - MAX_WIN_H // 2
        ws = jnp.where(is_first == 1, jnp.maximum(ws, 0), ws)
        ws = jnp.where(is_last == 1, jnp.minimum(ws, 0), ws)
        ro0 = ws - j + (MAX_WIN_H - 1)
        kstart = pl.multiple_of((ws + KV_HALO // GRID_W) * GRID_W, GRID_W)
        qstart = pl.multiple_of(j * GRID_W, GRID_W)
        for g in range(N_HEADS // HEAD_GROUP):
            cols = slice(g * gw, (g + 1) * gw)
            q4 = q_ref[pl.ds(qstart, GRID_W), cols]
            qexp = jnp.concatenate(
                [jnp.where(lane_head == hl, q4, jnp.zeros_like(q4)) for hl in range(HEAD_GROUP)],
                axis=0)
            kslab = kext_ref[pl.ds(kstart, win), cols]
            st = lax.dot_general(kslab, qexp, (((1,), (1,)), ((), ())),
                                 preferred_element_type=F32)
            st = st + jnp.concatenate([tab_ref[ro0 + w, g] for w in range(MAX_WIN_H)], axis=0)
            m = jnp.max(st, axis=0, keepdims=True)
            pexp = jnp.exp2(st - m)
            inv = 1.0 / jnp.sum(pexp, axis=0, keepdims=True)
            pn = (pexp * inv).astype(BF16)
            vslab = vext_ref[pl.ds(kstart, win), cols]
            o4 = lax.dot_general(pn, vslab, (((0,), (0,)), ((), ())),
                                 preferred_element_type=F32)
            out = o4[0:GRID_W]
            for hl in range(1, HEAD_GROUP):
                out = jnp.where(lane_head == hl, o4[hl * GRID_W:(hl + 1) * GRID_W], out)
            att_ref[pl.ds(qstart, GRID_W), cols] = out
        return carry

    lax.fori_loop(0, TILE_ROWS, row_body, 0, unroll=ATT_UNROLL)
    br_b = jnp.dot(att_ref[...].astype(BF16), wb_ref[...], preferred_element_type=F32)

    merged = ga_ref[...].astype(F32) * br_a + gb_ref[...].astype(F32) * br_b
    x_new = (_read_parts(refs[:n_parts], x_starts)
             + jnp.dot(merged.astype(BF16), wo_ref[...], preferred_element_type=F32))
    xo_ref[...] = x_new
    ms = jnp.mean(x_new * x_new, axis=-1, keepdims=True)
    h2 = x_new * lax.rsqrt(ms + EPS) * n2_ref[...]
    h2_ref[...] = h2.astype(h2_ref.dtype)

    if with_router:
        logits = jnp.dot(h2.astype(BF16), rw_ref[...], preferred_element_type=F32) + rb_ref[...]
        elane = lax.broadcasted_iota(jnp.int32, logits.shape, 1).astype(F32)
        logits = jnp.where(elane < N_EXPERTS, logits, -jnp.inf)
        m1 = jnp.max(logits, axis=1, keepdims=True)
        i1 = jnp.min(jnp.where(logits == m1, elane, float(LANES)), axis=1, keepdims=True)
        others = jnp.where(elane == i1, -jnp.inf, logits)
        m2 = jnp.max(others, axis=1, keepdims=True)
        i2 = jnp.min(jnp.where(others == m2, elane, float(LANES)), axis=1, keepdims=True)
        e2 = jnp.exp(m2 - m1)
        w1 = 1.0 / (1.0 + e2)
        w2 = e2 / (1.0 + e2)
        route = jnp.where(elane == 0, i1, jnp.where(elane == 1, i2, jnp.where(
            elane == 2, w1, jnp.where(elane == 3, w2, 0.0))))
        route_ref[...] = route[:, :ROUTE_W]
        route_t_ref[0] = route.T[:ROUTE_W, :]
        onehot = ((elane == i1) | (elane == i2)).astype(F32)

        @pl.when(i == 0)
        def _():
            tot_ref[...] = jnp.zeros(tot_ref.shape, F32)

        tot_ref[...] += jnp.broadcast_to(
            _round_up_rows(jnp.sum(onehot, axis=0, keepdims=True)), tot_ref.shape)


def _mixer(x_parts, vglu, q, k, v, ga, gb, first, last, conv_w, conv_b, ln_g, ln_b, table,
           w_a, w_b, w_o, norm2_g, router_w, router_b, *, with_router):
    x_starts, x_specs, n_tiles = _part_specs(x_parts)
    n_tok = n_tiles * TM
    conv_blocks = TM // CONV_HALO
    kv_blocks = TM // KV_HALO
    full2 = lambda shape: pl.BlockSpec(shape, lambda i, f, l: (0, 0))
    tile = lambda width: pl.BlockSpec((TM, width), lambda i, f, l: (i, 0))
    in_specs = x_specs + [
        tile(D_CONV),
        pl.BlockSpec((CONV_HALO, D_CONV), lambda i, f, l: (i * conv_blocks - 1 + f[i], 0)),
        pl.BlockSpec((CONV_HALO, D_CONV), lambda i, f, l: ((i + 1) * conv_blocks - l[i], 0)),
        tile(D_ATTN),
        tile(D_ATTN),
        pl.BlockSpec((KV_HALO, D_ATTN), lambda i, f, l: (i * kv_blocks - 1 + f[i], 0)),
        pl.BlockSpec((KV_HALO, D_ATTN), lambda i, f, l: ((i + 1) * kv_blocks - l[i], 0)),
        tile(D_ATTN),
        pl.BlockSpec((KV_HALO, D_ATTN), lambda i, f, l: (i * kv_blocks - 1 + f[i], 0)),
        pl.BlockSpec((KV_HALO, D_ATTN), lambda i, f, l: ((i + 1) * kv_blocks - l[i], 0)),
        tile(D_MODEL),
        tile(D_MODEL),
        full2((CONV_W + 1, D_CONV)),
        full2((1, D_CONV)), full2((1, D_CONV)), full2((1, D_CONV)),
        pl.BlockSpec(table.shape, lambda i, f, l: (0, 0, 0, 0)),
        full2((D_CONV, D_MODEL)), full2((D_ATTN, D_MODEL)), full2((D_MODEL, D_MODEL)),
        full2((1, D_MODEL)),
        full2((D_MODEL, LANES)), full2((1, LANES)),
    ]
    out_specs = [tile(D_MODEL), tile(D_MODEL)]
    out_shape = [jax.ShapeDtypeStruct((n_tok, D_MODEL), F32),
                 jax.ShapeDtypeStruct((n_tok, D_MODEL), F32 if with_router else BF16)]
    if with_router:
        out_specs += [tile(ROUTE_W),
                      pl.BlockSpec((1, ROUTE_W, TM), lambda i, f, l: (i, 0, 0)),
                      pl.BlockSpec((8, LANES), lambda i, f, l: (0, 0))]
        out_shape += [jax.ShapeDtypeStruct((n_tok, ROUTE_W), F32),
                      jax.ShapeDtypeStruct((n_tok // TM, ROUTE_W, TM), F32),
                      jax.ShapeDtypeStruct((8, LANES), F32)]
    ext_rows = TM + 2 * CONV_HALO
    scratch = [
        pltpu.VMEM((ext_rows, D_CONV), F32),
        pltpu.VMEM((SUBLANES, ext_rows - SUBLANES, LANES), F32),
        pltpu.VMEM((TM, D_CONV), F32),
        pltpu.VMEM((TM + 2 * KV_HALO, D_ATTN), BF16),
        pltpu.VMEM((TM + 2 * KV_HALO, D_ATTN), BF16),
        pltpu.VMEM((TM, D_ATTN), F32),
    ]
    return pl.pallas_call(
        functools.partial(_mixer_kernel, x_starts=x_starts, with_router=with_router),
        grid_spec=pltpu.PrefetchScalarGridSpec(
            num_scalar_prefetch=2, grid=(n_tiles,),
            in_specs=in_specs, out_specs=out_specs, scratch_shapes=scratch),
        out_shape=out_shape,
        compiler_params=pltpu.CompilerParams(
            dimension_semantics=("arbitrary",), vmem_limit_bytes=VMEM_LIMIT),
        name="mixer",
    )(first, last, *x_parts, vglu, vglu, vglu, q, k, k, k, v, v, v, ga, gb,
      conv_w, conv_b, ln_g, ln_b, table, w_a, w_b, w_o, norm2_g, router_w, router_b)


def _ffn_kernel(h_ref, x_ref, w1_ref, w3_ref, w2_ref, *rest, o_starts, side):
    o_refs = rest
    if side:
        rest[-1][...] = rest[0][...].astype(BF16)
        o_refs = rest[1:-1]
    h = h_ref[...]
    acc = x_ref[...]
    for c in range(0, D_FF, FF_CHUNK):
        a = jnp.dot(h, w1_ref[:, c:c + FF_CHUNK], preferred_element_type=F32)
        b = jnp.dot(h, w3_ref[:, c:c + FF_CHUNK], preferred_element_type=F32)
        act = (a * _sigmoid(a) * b).astype(BF16)
        acc = acc + jnp.dot(act, w2_ref[c:c + FF_CHUNK, :], preferred_element_type=F32)
    _write_parts(o_refs, o_starts, acc)


def _out_parts(n_tok, out_rows):
    shapes = [jax.ShapeDtypeStruct((r, D_MODEL), F32) for r in (out_rows or [n_tok])]
    starts, specs, _ = _part_specs(shapes)
    return shapes, starts, specs


def _ffn(h2, x, w1, w3, w2, out_rows=None, side_w=None):
    n_tok = x.shape[0]
    tile = lambda: pl.BlockSpec((TM, D_MODEL), lambda i: (i, 0))
    single = dict(pipeline_mode=pl.Buffered(1))
    o_shapes, o_starts, o_specs = _out_parts(n_tok, out_rows)
    plan = _side_cast(side_w, n_tok // TM) if side_w is not None else None
    side_args, side_spec, side_shape = [], [], []
    if plan is not None:
        side_args, side_spec, side_shape = [plan[0]], [plan[1]], [plan[2]]
    res = pl.pallas_call(
        functools.partial(_ffn_kernel, o_starts=o_starts, side=len(side_args)),
        grid=(n_tok // TM,),
        in_specs=[tile(), tile(),
                  pl.BlockSpec((D_MODEL, D_FF), lambda i: (0, 0), **single),
                  pl.BlockSpec((D_MODEL, D_FF), lambda i: (0, 0), **single),
                  pl.BlockSpec((D_FF, D_MODEL), lambda i: (0, 0), **single)] + side_spec,
        out_specs=o_specs + side_spec,
        out_shape=o_shapes + side_shape,
        compiler_params=pltpu.CompilerParams(
            dimension_semantics=("arbitrary",), vmem_limit_bytes=VMEM_LIMIT),
        name="ffn",
    )(h2, x, w1, w3, w2, *side_args)
    side_bf16 = None
    if side_w is not None:
        side_bf16 = res[-1].reshape(side_w.shape) if plan is not None else _to_bf16(side_w)
    return res[:len(o_shapes)], side_bf16


RUN_FIELDS = 3
PERM_ROWS = 2 * TM + N_EXPERTS * SUBLANES


def _slots_kernel(rt_ref, start_ref, loc_ref, runs_ref, run_ref):
    i = pl.program_id(0)

    @pl.when(i == 0)
    def _():
        run_ref[...] = jnp.zeros(run_ref.shape, F32)

    rt = rt_ref[0]
    eidx = lax.broadcasted_iota(jnp.int32, (N_EXPERTS, TM), 0).astype(F32)
    e1 = eidx == rt[0:1, :]
    e2 = eidx == rt[1:2, :]
    chosen = (e1 | e2).astype(F32)
    before = lax.broadcasted_iota(jnp.int32, (TM, TM), 0) < lax.broadcasted_iota(
        jnp.int32, (TM, TM), 1)
    rank = jnp.dot(chosen.astype(BF16), before.astype(BF16), preferred_element_type=F32)
    rows = _round_up_rows(jnp.sum(chosen, axis=1, keepdims=True))
    rows_b = jnp.broadcast_to(rows, (N_EXPERTS, LANES))
    expert = lax.broadcasted_iota(jnp.int32, (N_EXPERTS, LANES), 0)
    incl = rows_b
    for step in (1, 2, 4):
        incl = incl + jnp.where(expert >= step, pltpu.roll(incl, step, axis=0), 0.0)
    local = incl - rows_b
    pos = rank + local[:, 0:1]
    l1 = jnp.sum(jnp.where(e1, pos, 0.0), axis=0, keepdims=True)
    l2 = jnp.sum(jnp.where(e2, pos, 0.0), axis=0, keepdims=True)
    loc_ref[0] = jnp.concatenate([l1, l2], axis=0).astype(jnp.int32)
    lane = lax.broadcasted_iota(jnp.int32, (N_EXPERTS, LANES), 1)
    dest = start_ref[...] + run_ref[...]
    runs_ref[0] = jnp.where(lane == 0, dest, jnp.where(lane == 1, rows_b, local)).astype(jnp.int32)
    run_ref[...] += rows_b


def _slots(route_t, start):
    n_tiles = route_t.shape[0]
    return pl.pallas_call(
        _slots_kernel,
        grid=(n_tiles,),
        in_specs=[pl.BlockSpec((1, ROUTE_W, TM), lambda i: (i, 0, 0)),
                  pl.BlockSpec((N_EXPERTS, LANES), lambda i: (0, 0))],
        out_specs=[pl.BlockSpec((1, 2, TM), lambda i: (i, 0, 0)),
                   pl.BlockSpec((1, N_EXPERTS, LANES), lambda i: (i, 0, 0))],
        out_shape=[jax.ShapeDtypeStruct((n_tiles, 2, TM), jnp.int32),
                   jax.ShapeDtypeStruct((n_tiles, N_EXPERTS, LANES), jnp.int32)],
        scratch_shapes=[pltpu.VMEM((N_EXPERTS, LANES), F32)],
        compiler_params=pltpu.CompilerParams(dimension_semantics=("arbitrary",)),
        name="moe_slots",
    )(route_t, start)


def _run_copies(runs_ref, tile, make_copy, wait):
    for e in range(N_EXPERTS):
        base = (tile * N_EXPERTS + e) * RUN_FIELDS
        sorted_row, n, local_row = runs_ref[base], runs_ref[base + 1], runs_ref[base + 2]

        @pl.when(n > 0)
        def _():
            copy = make_copy(pl.multiple_of(sorted_row, SUBLANES),
                             pl.multiple_of(local_row, SUBLANES), pl.multiple_of(n, SUBLANES))
            copy.wait() if wait else copy.start()


def _perm_matrix(loc_row):
    rows = lax.broadcasted_iota(jnp.int32, (PERM_ROWS, TM), 0)
    return rows == loc_row


ZERO_ROWS = 64


def _dispatch_kernel(ts_ref, tl_ref, runs_ref, loc_ref, h_ref, xs_ref,
                     zero_ref, grouped_ref, sem, zsem):
    i = pl.program_id(0)

    def fill_padding(wait):
        def run(copy):
            copy.wait() if wait else copy.start()

        for e in range(N_EXPERTS):
            base = ts_ref[e]
            n = tl_ref[e]
            full = n // ZERO_ROWS

            def chunk(c, carry):
                off = pl.multiple_of(base + c * ZERO_ROWS, SUBLANES)
                run(pltpu.make_async_copy(zero_ref, xs_ref.at[pl.ds(off, ZERO_ROWS)], zsem))
                return carry

            lax.fori_loop(0, full, chunk, 0)
            rem_base = base + full * ZERO_ROWS
            for b in range(SUBLANES.bit_length() - 1, ZERO_ROWS.bit_length() - 1):
                size = 1 << b

                @pl.when(((n >> b) & 1) == 1)
                def _():
                    off = pl.multiple_of(rem_base + (n & (size - 1)), SUBLANES)
                    run(pltpu.make_async_copy(
                        zero_ref.at[pl.ds(0, size)], xs_ref.at[pl.ds(off, size)], zsem))

    @pl.when(i == 0)
    def _():
        zero_ref[...] = jnp.zeros(zero_ref.shape, F32)
        fill_padding(wait=False)

    slot = lax.rem(i, 2)
    loc = loc_ref[0]
    perm = (_perm_matrix(loc[0:1, :]) | _perm_matrix(loc[1:2, :])).astype(BF16)
    grouped_ref[slot] = jnp.dot(perm, h_ref[...].astype(BF16), preferred_element_type=F32)

    def run_copy(buf):
        def make(sorted_row, local_row, size):
            return pltpu.make_async_copy(grouped_ref.at[buf, pl.ds(local_row, size)],
                                         xs_ref.at[pl.ds(sorted_row, size)], sem.at[buf])
        return make

    _run_copies(runs_ref, i, run_copy(slot), wait=False)

    @pl.when(i > 0)
    def _():
        _run_copies(runs_ref, i - 1, run_copy(1 - slot), wait=True)

    @pl.when(i == pl.num_programs(0) - 1)
    def _():
        _run_copies(runs_ref, i, run_copy(slot), wait=True)

    @pl.when(i == 0)
    def _():
        fill_padding(wait=True)


def _dispatch(tail_start, tail_len, runs_flat, loc, h2, n_rows):
    n_tok = h2.shape[0]
    return pl.pallas_call(
        _dispatch_kernel,
        grid_spec=pltpu.PrefetchScalarGridSpec(
            num_scalar_prefetch=3, grid=(n_tok // TM,),
            in_specs=[pl.BlockSpec((1, 2, TM), lambda i, *_: (i, 0, 0)),
                      pl.BlockSpec((TM, D_MODEL), lambda i, *_: (i, 0))],
            out_specs=pl.BlockSpec(memory_space=pl.ANY),
            scratch_shapes=[pltpu.VMEM((ZERO_ROWS, D_MODEL), F32),
                            pltpu.VMEM((2, PERM_ROWS, D_MODEL), F32),
                            pltpu.SemaphoreType.DMA((2,)), pltpu.SemaphoreType.DMA]),
        out_shape=jax.ShapeDtypeStruct((n_rows, D_MODEL), F32),
        compiler_params=pltpu.CompilerParams(
            dimension_semantics=("arbitrary",), vmem_limit_bytes=VMEM_LIMIT),
        name="moe_dispatch",
    )(tail_start, tail_len, runs_flat, loc, h2)


def _expert_kernel(te_ref, na_ref, x_ref, w1_ref, w3_ref, w2_ref, y_ref):
    j = pl.program_id(0)

    @pl.when(j < na_ref[0])
    def _():
        h = x_ref[...].astype(BF16)
        acc = jnp.zeros((TM, D_MODEL), F32)
        for c in range(0, D_FF, FF_CHUNK):
            a = jnp.dot(h, w1_ref[0, :, c:c + FF_CHUNK], preferred_element_type=F32)
            b = jnp.dot(h, w3_ref[0, :, c:c + FF_CHUNK], preferred_element_type=F32)
            act = (a * _sigmoid(a) * b).astype(BF16)
            acc = acc + jnp.dot(act, w2_ref[0, c:c + FF_CHUNK, :], preferred_element_type=F32)
        y_ref[...] = acc

    @pl.when(j >= na_ref[0])
    def _():
        y_ref[...] = jnp.zeros(y_ref.shape, F32)


def _experts(tile_expert, n_active, xs, w1, w3, w2):
    n_tiles = xs.shape[0] // TM
    weights = lambda shape: pl.BlockSpec(shape, lambda j, te, na: (te[j], 0, 0))
    return pl.pallas_call(
        _expert_kernel,
        grid_spec=pltpu.PrefetchScalarGridSpec(
            num_scalar_prefetch=2, grid=(n_tiles,),
            in_specs=[pl.BlockSpec((TM, D_MODEL),
                                   lambda j, te, na: (jnp.minimum(j, na[0] - 1), 0)),
                      weights((1, D_MODEL, D_FF)), weights((1, D_MODEL, D_FF)),
                      weights((1, D_FF, D_MODEL))],
            out_specs=pl.BlockSpec((TM, D_MODEL), lambda j, te, na: (j, 0))),
        out_shape=jax.ShapeDtypeStruct(xs.shape, F32),
        compiler_params=pltpu.CompilerParams(
            dimension_semantics=("arbitrary",), vmem_limit_bytes=VMEM_LIMIT),
        name="moe_experts",
    )(tile_expert, n_active, xs, w1, w3, w2)


def _combine_kernel(runs_ref, loc_ref, x_ref, route_ref, y_ref, *refs, o_starts):
    stage_ref, sem = refs[-2:]
    i = pl.program_id(0)
    slot = lax.rem(i, 2)

    def run_copy(buf):
        def make(sorted_row, local_row, size):
            return pltpu.make_async_copy(y_ref.at[pl.ds(sorted_row, size)],
                                         stage_ref.at[buf, pl.ds(local_row, size)], sem.at[buf])
        return make

    @pl.when(i == 0)
    def _():
        stage_ref[...] = jnp.zeros(stage_ref.shape, F32)
        _run_copies(runs_ref, i, run_copy(slot), wait=False)

    @pl.when(i + 1 < pl.num_programs(0))
    def _():
        _run_copies(runs_ref, i + 1, run_copy(1 - slot), wait=False)

    _run_copies(runs_ref, i, run_copy(slot), wait=True)
    grouped = stage_ref[slot].astype(BF16)
    loc = loc_ref[0]
    route = route_ref[...]
    out = x_ref[...]
    for c in range(2):
        picked = lax.dot_general(_perm_matrix(loc[c:c + 1, :]).astype(BF16), grouped,
                                 (((0,), (0,)), ((), ())), preferred_element_type=F32)
        out = out + route[:, 2 + c:3 + c] * picked
    _write_parts(refs[:-2], o_starts, out)


def _combine(runs_flat, loc, x, route, y, out_rows=None):
    n_tok = x.shape[0]
    o_shapes, o_starts, o_specs = _out_parts(n_tok, out_rows)
    return pl.pallas_call(
        functools.partial(_combine_kernel, o_starts=o_starts),
        grid_spec=pltpu.PrefetchScalarGridSpec(
            num_scalar_prefetch=1, grid=(n_tok // TM,),
            in_specs=[pl.BlockSpec((1, 2, TM), lambda i, *_: (i, 0, 0)),
                      pl.BlockSpec((TM, D_MODEL), lambda i, *_: (i, 0)),
                      pl.BlockSpec((TM, ROUTE_W), lambda i, *_: (i, 0)),
                      pl.BlockSpec(memory_space=pl.ANY)],
            out_specs=o_specs,
            scratch_shapes=[pltpu.VMEM((2, PERM_ROWS, D_MODEL), F32),
                            pltpu.SemaphoreType.DMA((2,))]),
        out_shape=o_shapes,
        compiler_params=pltpu.CompilerParams(
            dimension_semantics=("arbitrary",), vmem_limit_bytes=VMEM_LIMIT),
        name="moe_combine",
    )(runs_flat, loc, x, route, y)


def _moe(h2, x, route, route_t, totals, w1, w3, w2, out_rows=None):
    n_tok = x.shape[0]
    tot = totals[0, :N_EXPERTS].astype(jnp.int32)
    padded = (tot + TM - 1) // TM * TM
    end = jnp.cumsum(padded)
    start = end - padded
    max_rows = 2 * n_tok + (n_tok // TM) * N_EXPERTS * (SUBLANES - 1) + N_EXPERTS * (TM - 1)
    n_tiles = max_rows // TM
    n_active = end[-1] // TM
    tile = jnp.minimum(jnp.arange(n_tiles, dtype=jnp.int32), n_active - 1)
    tile_expert = jnp.sum(tile[:, None] * TM >= end[None, :], axis=1).astype(jnp.int32)
    start_b = jnp.broadcast_to(start.astype(F32)[:, None], (N_EXPERTS, LANES))
    tail_start = (start + tot).astype(jnp.int32)
    tail_len = (end.at[-1].set(n_tiles * TM) - tail_start).astype(jnp.int32)

    loc, runs = _slots(route_t, start_b)
    runs_flat = runs[:, :, :RUN_FIELDS].reshape(-1)
    xs = _dispatch(tail_start, tail_len, runs_flat, loc, h2, n_tiles * TM)
    y = _experts(tile_expert, n_active.reshape(1).astype(jnp.int32), xs, w1, w3, w2)
    return _combine(runs_flat, loc, x, route, y, out_rows)


def _tile_edges(seq_lens):
    first, last = [], []
    for n in seq_lens:
        t = n // TM
        first += [1] + [0] * (t - 1)
        last += [0] * (t - 1) + [1]
    return jnp.asarray(first, jnp.int32), jnp.asarray(last, jnp.int32)


def kernel(x_prompt, x_sample, norm1_g, w_in, b_gate, conv_w, conv_b, conv_ln_g, conv_ln_b,
           q_norm_g, k_norm_g, rpb, w_proj_a, w_proj_b, w_out, norm2_g, ffn_w1, ffn_w3, ffn_w2,
           router_w, router_b, moe_w1, moe_w3, moe_w2):
    depth = w_in.shape[0]
    seq_lens = [x_prompt.shape[1]] * x_prompt.shape[0] + [x_sample.shape[1]] * x_sample.shape[0]
    assert all(n % TM == 0 and n // GRID_W >= 2 * TILE_ROWS for n in seq_lens)
    first, last = _tile_edges(seq_lens)
    x_parts = [x_prompt.reshape(-1, D_MODEL), x_sample.reshape(-1, D_MODEL)]
    out_rows = [p.shape[0] for p in x_parts]

    head_id = jnp.arange(D_ATTN) // HEAD_DIM
    head_mean = jnp.where(head_id[:, None] == head_id[None, :], 1.0 / HEAD_DIM, 0.0).astype(BF16)
    row = lambda a: a.reshape(1, -1).astype(F32)

    moe_bf16 = {}
    for l in range(depth):
        final = out_rows if l == depth - 1 else None
        j = l // 2
        moe = l % 2 == 1
        feeds_moe = not moe and l + 1 < depth
        table = _bias_table(rpb[l])
        side_w = moe_w2[j] if moe else (moe_w1[j] if feeds_moe else None)
        (vglu, q, k, v, ga, gb), side = _in_proj(
            x_parts, row(norm1_g[l]), _to_bf16(w_in, l), row(b_gate[l]),
            row(jnp.tile(q_norm_g[l], N_HEADS)), row(jnp.tile(k_norm_g[l], N_HEADS)), head_mean,
            side_w=side_w)
        if side is not None:
            moe_bf16["w2" if moe else "w1", j] = side
        if moe:
            rw = jnp.pad(router_w[j], ((0, 0), (0, LANES - N_EXPERTS))).astype(BF16)
            rb = jnp.pad(row(router_b[j]), ((0, 0), (0, LANES - N_EXPERTS)))
        else:
            rw = jnp.zeros((D_MODEL, LANES), BF16)
            rb = jnp.zeros((1, LANES), F32)
        cw = jnp.pad(conv_w[l], ((0, 1), (0, 0)))
        outs = _mixer(
            x_parts, vglu, q, k, v, ga, gb, first, last, cw, row(conv_b[l]), row(conv_ln_g[l]),
            row(conv_ln_b[l]), table, _to_bf16(w_proj_a, l), _to_bf16(w_proj_b, l),
            _to_bf16(w_out, l), row(norm2_g[l]), rw, rb, with_router=moe)
        if moe:
            x, h2, route, route_t, totals = outs
            w1 = moe_bf16.get(("w1", j))
            w3 = moe_bf16.get(("w3", j))
            res = _moe(h2, x, route, route_t, totals,
                       _to_bf16(moe_w1[j]) if w1 is None else w1,
                       _to_bf16(moe_w3[j]) if w3 is None else w3,
                       moe_bf16["w2", j], final)
        else:
            x, h2 = outs
            res, side = _ffn(h2, x, _to_bf16(ffn_w1, j), _to_bf16(ffn_w3, j),
                             _to_bf16(ffn_w2, j), final,
                             side_w=moe_w3[j] if feeds_moe else None)
            if side is not None:
                moe_bf16["w3", j] = side
        x_parts = list(res)

    return (x_parts[0].reshape(x_prompt.shape), x_parts[1].reshape(x_sample.shape))
```

```python
import functools

import jax
import jax.numpy as jnp
from jax import lax
from jax.experimental import pallas as pl
from jax.experimental.pallas import tpu as pltpu

D_MODEL = 1024
GRID_W = 64
D_CONV = 512
CONV_W = 31
N_HEADS = 8
HEAD_DIM = 64
D_ATTN = N_HEADS * HEAD_DIM
MAX_WIN_H = 8
WIN_W = 16
D_FF = 2816
N_EXPERTS = 8
TOP_K = 2
EPS = 1e-6

LANES = 128
SUBLANES = 8
BF16_ROWS = 16
TM = 512
TILE_ROWS = TM // GRID_W
KV_HALO = 256
CONV_HALO = 16
CONV_PAD = CONV_W // 2
CONV_ROWS = 256
N_RO = 2 * MAX_WIN_H - 1
N_CO = 2 * WIN_W - 1
HEAD_GROUP = 4
ROUTE_W = 8
ATT_UNROLL = 8
NEG_BIG = -1e30
LOG2E = 1.4426950408889634
FF_CHUNK = 256
VMEM_LIMIT = 56 * 1024 * 1024

F32 = jnp.float32
BF16 = jnp.bfloat16


def _sigmoid(x):
    return 1.0 / (1.0 + jnp.exp(-x))


def _round_up_rows(count):
    return jnp.floor((count + (SUBLANES - 1)) * (1.0 / SUBLANES)) * SUBLANES


def _part_specs(parts, width=D_MODEL):
    starts, specs, tile0 = [], [], 0
    for a in parts:
        n = a.shape[0] // TM
        specs.append(pl.BlockSpec(
            (TM, width), lambda i, *_, s=tile0, n=n: (jnp.clip(i - s, 0, n - 1), 0)))
        starts.append(tile0)
        tile0 += n
    return tuple(starts), specs, tile0


def _read_parts(refs, starts):
    i = pl.program_id(0)
    x = refs[0][...]
    for ref, s in zip(refs[1:], starts[1:]):
        x = jnp.where(i >= s, ref[...], x)
    return x


def _write_parts(refs, starts, value):
    i = pl.program_id(0)
    ends = starts[1:] + (None,)
    for ref, s, e in zip(refs, starts, ends):
        owns = i >= s if e is None else (i >= s) & (i < e)

        @pl.when(owns)
        def _():
            ref[...] = value


CAST_ROWS = 512
CAST_BLOCK_ELEMS = 3 * 1024 * 1024


def _cast_kernel(w_ref, o_ref):
    o_ref[...] = w_ref[...].astype(BF16)


def _to_bf16(w, take=None):
    lead = w.shape[:-2]
    k, n = w.shape[-2:]
    w3 = w.reshape((-1, k, n))
    n_mat = w3.shape[0] if take is None else 1
    first = 0 if take is None else take
    rows = k if k * n <= CAST_BLOCK_ELEMS else next(
        (r for r in (CAST_ROWS, CAST_ROWS // 2, CAST_ROWS // 4) if k % r == 0), k)
    out = pl.pallas_call(
        _cast_kernel,
        grid=(n_mat, k // rows),
        in_specs=[pl.BlockSpec((1, rows, n), lambda e, r: (first + e, r, 0))],
        out_specs=pl.BlockSpec((1, rows, n), lambda e, r: (e, r, 0)),
        out_shape=jax.ShapeDtypeStruct((n_mat, k, n), BF16),
        compiler_params=pltpu.CompilerParams(
            dimension_semantics=("arbitrary", "arbitrary"), vmem_limit_bytes=VMEM_LIMIT),
        name="to_bf16",
    )(w3)
    return out.reshape((k, n) if take is not None else lead + (k, n))


def _side_cast(w, n_steps):
    w2 = w.reshape(-1, w.shape[-1])
    rows = w2.shape[0] // n_steps
    if w2.shape[0] % n_steps or rows % BF16_ROWS:
        return None
    spec = pl.BlockSpec((rows, w2.shape[1]), lambda i, *_: (i, 0))
    return w2, spec, jax.ShapeDtypeStruct(w2.shape, BF16)


def _bias_table_kernel(rpb_ref, out_ref):
    ro = pl.program_id(0)
    shape = (GRID_W, HEAD_GROUP * GRID_W)
    kcol = lax.broadcasted_iota(jnp.int32, shape, 0)
    lane = lax.broadcasted_iota(jnp.int32, shape, 1)
    head_local = lane // GRID_W
    qcol = lane - head_local * GRID_W
    start_c = jnp.clip(qcol - WIN_W // 2, 0, GRID_W - WIN_W)
    valid = (kcol >= start_c) & (kcol < start_c + WIN_W)
    co = jnp.clip(kcol - qcol + (WIN_W - 1), 0, 2 * WIN_W - 2)
    for g in range(N_HEADS // HEAD_GROUP):
        acc = jnp.zeros(shape, F32)
        for d in range(N_CO):
            val = jnp.zeros(shape, F32)
            for hl in range(HEAD_GROUP):
                h = g * HEAD_GROUP + hl
                val = jnp.where(head_local == hl, rpb_ref[(h * N_RO + ro) * N_CO + d], val)
            acc = jnp.where(co == d, val, acc)
        out_ref[0, g] = jnp.where(valid, acc * LOG2E, NEG_BIG)


def _bias_table(rpb_l):
    n_groups = N_HEADS // HEAD_GROUP
    width = HEAD_GROUP * GRID_W
    return pl.pallas_call(
        _bias_table_kernel,
        grid=(N_RO,),
        in_specs=[pl.BlockSpec(memory_space=pltpu.SMEM)],
        out_specs=pl.BlockSpec((1, n_groups, GRID_W, width), lambda r: (r, 0, 0, 0)),
        out_shape=jax.ShapeDtypeStruct((N_RO, n_groups, GRID_W, width), F32),
        name="bias_table",
    )(rpb_l.reshape(-1))


def _in_proj_kernel(*refs, x_starts, side):
    n_parts = len(x_starts)
    g_ref, w_ref, bg_ref, qg_ref, kg_ref, hm_ref = refs[n_parts:n_parts + 6]
    outs = refs[n_parts + 6 + side:]
    vglu_ref, q_ref, k_ref, v_ref, ga_ref, gb_ref = outs[:6]
    if side:
        outs[6][...] = refs[n_parts + 6][...].astype(BF16)
    x = _read_parts(refs[:n_parts], x_starts)
    ms = jnp.mean(x * x, axis=-1, keepdims=True)
    h = (x * lax.rsqrt(ms + EPS) * g_ref[...]).astype(BF16)

    def proj(lo, hi):
        return jnp.dot(h, w_ref[:, lo:hi], preferred_element_type=F32)

    def head_norm(t, gain):
        msq = jnp.dot((t * t).astype(BF16), hm_ref[...], preferred_element_type=F32)
        return t * lax.rsqrt(msq + EPS) * gain

    a = proj(0, D_CONV)
    gate = proj(D_CONV, 2 * D_CONV)
    vglu_ref[...] = (a * _sigmoid(gate)).astype(BF16)
    o = 2 * D_CONV
    q_ref[...] = (head_norm(proj(o, o + D_ATTN), qg_ref[...])
                  * (HEAD_DIM ** -0.5 * LOG2E)).astype(BF16)
    k_ref[...] = head_norm(proj(o + D_ATTN, o + 2 * D_ATTN), kg_ref[...]).astype(BF16)
    v_ref[...] = proj(o + 2 * D_ATTN, o + 3 * D_ATTN).astype(BF16)
    o += 3 * D_ATTN
    bg = bg_ref[...]
    ga_ref[...] = _sigmoid(proj(o, o + D_MODEL) + bg[:, :D_MODEL]).astype(BF16)
    gb_ref[...] = _sigmoid(proj(o + D_MODEL, o + 2 * D_MODEL) + bg[:, D_MODEL:]).astype(BF16)


def _in_proj(x_parts, norm_g, w_in, b_gate, q_norm_g, k_norm_g, head_mean, side_w=None):
    x_starts, x_specs, n_tiles = _part_specs(x_parts)
    n_tok = n_tiles * TM
    d_in = w_in.shape[1]
    full = lambda shape: pl.BlockSpec(shape, lambda i: (0, 0))
    tile = lambda width: pl.BlockSpec((TM, width), lambda i: (i, 0))
    out = lambda width: jax.ShapeDtypeStruct((n_tok, width), BF16)
    plan = _side_cast(side_w, n_tiles) if side_w is not None else None
    side_args, side_in, side_out, side_shape = [], [], [], []
    if plan is not None:
        side_args, side_in, side_out, side_shape = [plan[0]], [plan[1]], [plan[1]], [plan[2]]
    res = pl.pallas_call(
        functools.partial(_in_proj_kernel, x_starts=x_starts, side=len(side_args)),
        grid=(n_tiles,),
        in_specs=x_specs + [full((1, D_MODEL)), full((D_MODEL, d_in)),
                            full((1, 2 * D_MODEL)), full((1, D_ATTN)), full((1, D_ATTN)),
                            full((D_ATTN, D_ATTN))] + side_in,
        out_specs=[tile(D_CONV), tile(D_ATTN), tile(D_ATTN), tile(D_ATTN),
                   tile(D_MODEL), tile(D_MODEL)] + side_out,
        out_shape=[out(D_CONV), out(D_ATTN), out(D_ATTN), out(D_ATTN),
                   out(D_MODEL), out(D_MODEL)] + side_shape,
        compiler_params=pltpu.CompilerParams(
            dimension_semantics=("arbitrary",), vmem_limit_bytes=VMEM_LIMIT),
        name="in_proj",
    )(*x_parts, norm_g, w_in, b_gate, q_norm_g, k_norm_g, head_mean, *side_args)
    side_bf16 = None
    if side_w is not None:
        side_bf16 = res[6].reshape(side_w.shape) if plan is not None else _to_bf16(side_w)
    return res[:6], side_bf16


def _mixer_kernel(first_ref, last_ref, *refs, x_starts, with_router):
    n_parts = len(x_starts)
    (vg_ref, vgp_ref, vgn_ref, q_ref,
     k_ref, kp_ref, kn_ref, v_ref, vp_ref, vn_ref,
     ga_ref, gb_ref, cw_ref, cb_ref, lg_ref, lb_ref, tab_ref,
     wa_ref, wb_ref, wo_ref, n2_ref, rw_ref, rb_ref) = refs[n_parts:n_parts + 23]
    rest = refs[n_parts + 23:]
    if with_router:
        xo_ref, h2_ref, route_ref, route_t_ref, tot_ref = rest[:5]
    else:
        xo_ref, h2_ref = rest[:2]
    ext_ref, sh_ref, conv_ref, kext_ref, vext_ref, att_ref = rest[-6:]
    i = pl.program_id(0)
    is_first = first_ref[i]
    is_last = last_ref[i]

    keep_prev = (1 - is_first).astype(F32)
    keep_next = (1 - is_last).astype(F32)
    ext_ref[0:CONV_HALO, :] = vgp_ref[...].astype(F32) * keep_prev
    ext_ref[CONV_HALO:CONV_HALO + TM, :] = vg_ref[...].astype(F32)
    ext_ref[CONV_HALO + TM:, :] = vgn_ref[...].astype(F32) * keep_next
    n_sh_rows = sh_ref.shape[1]
    row_chunk = CONV_ROWS
    for cb in range(D_CONV // LANES):
        cols = slice(cb * LANES, (cb + 1) * LANES)
        for b in range(SUBLANES):
            sh_ref[b] = ext_ref[b:b + n_sh_rows, cols]

        def conv_rows(rc, carry, cols=cols):
            base = pl.multiple_of(rc * row_chunk, row_chunk)
            acc = jnp.zeros((row_chunk, LANES), F32)
            for j in range(CONV_W):
                off = CONV_HALO - CONV_PAD + j
                a8, b = off // SUBLANES, off % SUBLANES
                acc = acc + (sh_ref[b, pl.ds(base + SUBLANES * a8, row_chunk), :]
                             * cw_ref[j:j + 1, cols])
            conv_ref[pl.ds(base, row_chunk), cols] = acc
            return carry

        lax.fori_loop(0, TM // row_chunk, conv_rows, 0)
    y = conv_ref[...] + cb_ref[...]
    mu = jnp.mean(y, axis=-1, keepdims=True)
    yc = y - mu
    var = jnp.mean(yc * yc, axis=-1, keepdims=True)
    yn = yc * lax.rsqrt(var + EPS) * lg_ref[...] + lb_ref[...]
    act = (yn * _sigmoid(yn)).astype(BF16)
    br_a = jnp.dot(act, wa_ref[...], preferred_element_type=F32)

    kext_ref[0:KV_HALO, :] = kp_ref[...]
    kext_ref[KV_HALO:KV_HALO + TM, :] = k_ref[...]
    kext_ref[KV_HALO + TM:, :] = kn_ref[...]
    vext_ref[0:KV_HALO, :] = vp_ref[...]
    vext_ref[KV_HALO:KV_HALO + TM, :] = v_ref[...]
    vext_ref[KV_HALO + TM:, :] = vn_ref[...]
    gw = HEAD_GROUP * HEAD_DIM
    lane_head = lax.broadcasted_iota(jnp.int32, (GRID_W, gw), 1) // HEAD_DIM
    win = MAX_WIN_H * GRID_W

    def row_body(j, carry):
        ws = j - MAX_WIN_H // 2
        ws = jnp.where(is_first == 1, jnp.maximum(ws, 0), ws)
        ws = jnp.where(is_last == 1, jnp.minimum(ws, 0), ws)
        ro0 = ws - j + (MAX_WIN_H - 1)
        kstart = pl.multiple_of((ws + KV_HALO // GRID_W) * GRID_W, GRID_W)
        qstart = pl.multiple_of(j * GRID_W, GRID_W)
        for g in range(N_HEADS // HEAD_GROUP):
            cols = slice(g * gw, (g + 1) * gw)
            q4 = q_ref[pl.ds(qstart, GRID_W), cols]
            qexp = jnp.concatenate(
                [jnp.where(lane_head == hl, q4, jnp.zeros_like(q4)) for hl in range(HEAD_GROUP)],
                axis=0)
            kslab = kext_ref[pl.ds(kstart, win), cols]
            st = lax.dot_general(kslab, qexp, (((1,), (1,)), ((), ())),
                                 preferred_element_type=F32)
            st = st + jnp.concatenate([tab_ref[ro0 + w, g] for w in range(MAX_WIN_H)], axis=0)
            m = jnp.max(st, axis=0, keepdims=True)
            pexp = jnp.exp2(st - m)
            inv = 1.0 / jnp.sum(pexp, axis=0, keepdims=True)
            pn = (pexp * inv).astype(BF16)
            vslab = vext_ref[pl.ds(kstart, win), cols]
            o4 = lax.dot_general(pn, vslab, (((0,), (0,)), ((), ())),
                                 preferred_element_type=F32)
            out = o4[0:GRID_W]
            for hl in range(1, HEAD_GROUP):
                out = jnp.where(lane_head == hl, o4[hl * GRID_W:(hl + 1) * GRID_W], out)
            att_ref[pl.ds(qstart, GRID_W), cols] = out
        return carry

    lax.fori_loop(0, TILE_ROWS, row_body, 0, unroll=ATT_UNROLL)
    br_b = jnp.dot(att_ref[...].astype(BF16), wb_ref[...], preferred_element_type=F32)

    merged = ga_ref[...].astype(F32) * br_a + gb_ref[...].astype(F32) * br_b
    x_new = (_read_parts(refs[:n_parts], x_starts)
             + jnp.dot(merged.astype(BF16), wo_ref[...], preferred_element_type=F32))
    xo_ref[...] = x_new
    ms = jnp.mean(x_new * x_new, axis=-1, keepdims=True)
    h2 = x_new * lax.rsqrt(ms + EPS) * n2_ref[...]
    h2_ref[...] = h2.astype(h2_ref.dtype)

    if with_router:
        logits = jnp.dot(h2.astype(BF16), rw_ref[...], preferred_element_type=F32) + rb_ref[...]
        elane = lax.broadcasted_iota(jnp.int32, logits.shape, 1).astype(F32)
        logits = jnp.where(elane < N_EXPERTS, logits, -jnp.inf)
        m1 = jnp.max(logits, axis=1, keepdims=True)
        i1 = jnp.min(jnp.where(logits == m1, elane, float(LANES)), axis=1, keepdims=True)
        others = jnp.where(elane == i1, -jnp.inf, logits)
        m2 = jnp.max(others, axis=1, keepdims=True)
        i2 = jnp.min(jnp.where(others == m2, elane, float(LANES)), axis=1, keepdims=True)
        e2 = jnp.exp(m2 - m1)
        w1 = 1.0 / (1.0 + e2)
        w2 = e2 / (1.0 + e2)
        route = jnp.where(elane == 0, i1, jnp.where(elane == 1, i2, jnp.where(
            elane == 2, w1, jnp.where(elane == 3, w2, 0.0))))
        route_ref[...] = route[:, :ROUTE_W]
        route_t_ref[0] = route.T[:ROUTE_W, :]
        onehot = ((elane == i1) | (elane == i2)).astype(F32)

        @pl.when(i == 0)
        def _():
            tot_ref[...] = jnp.zeros(tot_ref.shape, F32)

        tot_ref[...] += jnp.broadcast_to(
            _round_up_rows(jnp.sum(onehot, axis=0, keepdims=True)), tot_ref.shape)


def _mixer(x_parts, vglu, q, k, v, ga, gb, first, last, conv_w, conv_b, ln_g, ln_b, table,
           w_a, w_b, w_o, norm2_g, router_w, router_b, *, with_router):
    x_starts, x_specs, n_tiles = _part_specs(x_parts)
    n_tok = n_tiles * TM
    conv_blocks = TM // CONV_HALO
    kv_blocks = TM // KV_HALO
    full2 = lambda shape: pl.BlockSpec(shape, lambda i, f, l: (0, 0))
    tile = lambda width: pl.BlockSpec((TM, width), lambda i, f, l: (i, 0))
    in_specs = x_specs + [
        tile(D_CONV),
        pl.BlockSpec((CONV_HALO, D_CONV), lambda i, f, l: (i * conv_blocks - 1 + f[i], 0)),
        pl.BlockSpec((CONV_HALO, D_CONV), lambda i, f, l: ((i + 1) * conv_blocks - l[i], 0)),
        tile(D_ATTN),
        tile(D_ATTN),
        pl.BlockSpec((KV_HALO, D_ATTN), lambda i, f, l: (i * kv_blocks - 1 + f[i], 0)),
        pl.BlockSpec((KV_HALO, D_ATTN), lambda i, f, l: ((i + 1) * kv_blocks - l[i], 0)),
        tile(D_ATTN),
        pl.BlockSpec((KV_HALO, D_ATTN), lambda i, f, l: (i * kv_blocks - 1 + f[i], 0)),
        pl.BlockSpec((KV_HALO, D_ATTN), lambda i, f, l: ((i + 1) * kv_blocks - l[i], 0)),
        tile(D_MODEL),
        tile(D_MODEL),
        full2((CONV_W + 1, D_CONV)),
        full2((1, D_CONV)), full2((1, D_CONV)), full2((1, D_CONV)),
        pl.BlockSpec(table.shape, lambda i, f, l: (0, 0, 0, 0)),
        full2((D_CONV, D_MODEL)), full2((D_ATTN, D_MODEL)), full2((D_MODEL, D_MODEL)),
        full2((1, D_MODEL)),
        full2((D_MODEL, LANES)), full2((1, LANES)),
    ]
    out_specs = [tile(D_MODEL), tile(D_MODEL)]
    out_shape = [jax.ShapeDtypeStruct((n_tok, D_MODEL), F32),
                 jax.ShapeDtypeStruct((n_tok, D_MODEL), F32 if with_router else BF16)]
    if with_router:
        out_specs += [tile(ROUTE_W),
                      pl.BlockSpec((1, ROUTE_W, TM), lambda i, f, l: (i, 0, 0)),
                      pl.BlockSpec((N_EXPERTS, LANES), lambda i, f, l: (0, 0))]
        out_shape += [jax.ShapeDtypeStruct((n_tok, ROUTE_W), F32),
                      jax.ShapeDtypeStruct((n_tok // TM, ROUTE_W, TM), F32),
                      jax.ShapeDtypeStruct((N_EXPERTS, LANES), F32)]
    ext_rows = TM + 2 * CONV_HALO
    scratch = [
        pltpu.VMEM((ext_rows, D_CONV), F32),
        pltpu.VMEM((SUBLANES, ext_rows - SUBLANES, LANES), F32),
        pltpu.VMEM((TM, D_CONV), F32),
        pltpu.VMEM((TM + 2 * KV_HALO, D_ATTN), BF16),
        pltpu.VMEM((TM + 2 * KV_HALO, D_ATTN), BF16),
        pltpu.VMEM((TM, D_ATTN), F32),
    ]
    return pl.pallas_call(
        functools.partial(_mixer_kernel, x_starts=x_starts, with_router=with_router),
        grid_spec=pltpu.PrefetchScalarGridSpec(
            num_scalar_prefetch=2, grid=(n_tiles,),
            in_specs=in_specs, out_specs=out_specs, scratch_shapes=scratch),
        out_shape=out_shape,
        compiler_params=pltpu.CompilerParams(
            dimension_semantics=("arbitrary",), vmem_limit_bytes=VMEM_LIMIT),
        name="mixer",
    )(first, last, *x_parts, vglu, vglu, vglu, q, k, k, k, v, v, v, ga, gb,
      conv_w, conv_b, ln_g, ln_b, table, w_a, w_b, w_o, norm2_g, router_w, router_b)


def _ffn_kernel(h_ref, x_ref, w1_ref, w3_ref, w2_ref, *rest, o_starts, side):
    o_refs = rest
    if side:
        rest[-1][...] = rest[0][...].astype(BF16)
        o_refs = rest[1:-1]
    h = h_ref[...]
    acc = x_ref[...]
    for c in range(0, D_FF, FF_CHUNK):
        a = jnp.dot(h, w1_ref[:, c:c + FF_CHUNK], preferred_element_type=F32)
        b = jnp.dot(h, w3_ref[:, c:c + FF_CHUNK], preferred_element_type=F32)
        act = (a * _sigmoid(a) * b).astype(BF16)
        acc = acc + jnp.dot(act, w2_ref[c:c + FF_CHUNK, :], preferred_element_type=F32)
    _write_parts(o_refs, o_starts, acc)


def _out_parts(n_tok, out_rows):
    shapes = [jax.ShapeDtypeStruct((r, D_MODEL), F32) for r in (out_rows or [n_tok])]
    starts, specs, _ = _part_specs(shapes)
    return shapes, starts, specs


def _ffn(h2, x, w1, w3, w2, out_rows=None, side_w=None):
    n_tok = x.shape[0]
    tile = lambda: pl.BlockSpec((TM, D_MODEL), lambda i: (i, 0))
    single = dict(pipeline_mode=pl.Buffered(1))
    o_shapes, o_starts, o_specs = _out_parts(n_tok, out_rows)
    plan = _side_cast(side_w, n_tok // TM) if side_w is not None else None
    side_args, side_spec, side_shape = [], [], []
    if plan is not None:
        side_args, side_spec, side_shape = [plan[0]], [plan[1]], [plan[2]]
    res = pl.pallas_call(
        functools.partial(_ffn_kernel, o_starts=o_starts, side=len(side_args)),
        grid=(n_tok // TM,),
        in_specs=[tile(), tile(),
                  pl.BlockSpec((D_MODEL, D_FF), lambda i: (0, 0), **single),
                  pl.BlockSpec((D_MODEL, D_FF), lambda i: (0, 0), **single),
                  pl.BlockSpec((D_FF, D_MODEL), lambda i: (0, 0), **single)] + side_spec,
        out_specs=o_specs + side_spec,
        out_shape=o_shapes + side_shape,
        compiler_params=pltpu.CompilerParams(
            dimension_semantics=("arbitrary",), vmem_limit_bytes=VMEM_LIMIT),
        name="ffn",
    )(h2, x, w1, w3, w2, *side_args)
    side_bf16 = None
    if side_w is not None:
        side_bf16 = res[-1].reshape(side_w.shape) if plan is not None else _to_bf16(side_w)
    return res[:len(o_shapes)], side_bf16


RUN_FIELDS = 3
PERM_ROWS = TOP_K * TM + N_EXPERTS * SUBLANES


def _slots_kernel(rt_ref, start_ref, loc_ref, runs_ref, run_ref):
    i = pl.program_id(0)

    @pl.when(i == 0)
    def _():
        run_ref[...] = jnp.zeros(run_ref.shape, F32)

    rt = rt_ref[0]
    eidx = lax.broadcasted_iota(jnp.int32, (N_EXPERTS, TM), 0).astype(F32)
    e1 = eidx == rt[0:1, :]
    e2 = eidx == rt[1:2, :]
    chosen = (e1 | e2).astype(F32)
    before = lax.broadcasted_iota(jnp.int32, (TM, TM), 0) < lax.broadcasted_iota(
        jnp.int32, (TM, TM), 1)
    rank = jnp.dot(chosen.astype(BF16), before.astype(BF16), preferred_element_type=F32)
    rows = _round_up_rows(jnp.sum(chosen, axis=1, keepdims=True))
    rows_b = jnp.broadcast_to(rows, (N_EXPERTS, LANES))
    expert = lax.broadcasted_iota(jnp.int32, (N_EXPERTS, LANES), 0)
    incl = rows_b
    for step in (1, 2, 4):
        incl = incl + jnp.where(expert >= step, pltpu.roll(incl, step, axis=0), 0.0)
    local = incl - rows_b
    pos = rank + local[:, 0:1]
    l1 = jnp.sum(jnp.where(e1, pos, 0.0), axis=0, keepdims=True)
    l2 = jnp.sum(jnp.where(e2, pos, 0.0), axis=0, keepdims=True)
    loc_ref[0] = jnp.concatenate([l1, l2], axis=0).astype(jnp.int32)
    lane = lax.broadcasted_iota(jnp.int32, (N_EXPERTS, LANES), 1)
    dest = start_ref[...] + run_ref[...]
    runs_ref[0] = jnp.where(lane == 0, dest, jnp.where(lane == 1, rows_b, local)).astype(jnp.int32)
    run_ref[...] += rows_b


def _slots(route_t, start):
    n_tiles = route_t.shape[0]
    return pl.pallas_call(
        _slots_kernel,
        grid=(n_tiles,),
        in_specs=[pl.BlockSpec((1, ROUTE_W, TM), lambda i: (i, 0, 0)),
                  pl.BlockSpec((N_EXPERTS, LANES), lambda i: (0, 0))],
        out_specs=[pl.BlockSpec((1, TOP_K, TM), lambda i: (i, 0, 0)),
                   pl.BlockSpec((1, N_EXPERTS, LANES), lambda i: (i, 0, 0))],
        out_shape=[jax.ShapeDtypeStruct((n_tiles, TOP_K, TM), jnp.int32),
                   jax.ShapeDtypeStruct((n_tiles, N_EXPERTS, LANES), jnp.int32)],
        scratch_shapes=[pltpu.VMEM((N_EXPERTS, LANES), F32)],
        compiler_params=pltpu.CompilerParams(dimension_semantics=("arbitrary",)),
        name="moe_slots",
    )(route_t, start)


def _run_copies(runs_ref, tile, make_copy, wait):
    for e in range(N_EXPERTS):
        base = (tile * N_EXPERTS + e) * RUN_FIELDS
        sorted_row, n, local_row = runs_ref[base], runs_ref[base + 1], runs_ref[base + 2]

        @pl.when(n > 0)
        def _():
            copy = make_copy(pl.multiple_of(sorted_row, SUBLANES),
                             pl.multiple_of(local_row, SUBLANES), pl.multiple_of(n, SUBLANES))
            copy.wait() if wait else copy.start()


def _perm_matrix(loc_row):
    rows = lax.broadcasted_iota(jnp.int32, (PERM_ROWS, TM), 0)
    return rows == loc_row


ZERO_ROWS = 64


def _dispatch_kernel(ts_ref, tl_ref, runs_ref, loc_ref, h_ref, xs_ref,
                     zero_ref, grouped_ref, sem, zsem):
    i = pl.program_id(0)

    def fill_padding(wait):
        def run(copy):
            copy.wait() if wait else copy.start()

        for e in range(N_EXPERTS):
            base = ts_ref[e]
            n = tl_ref[e]
            full = n // ZERO_ROWS

            def chunk(c, carry):
                off = pl.multiple_of(base + c * ZERO_ROWS, SUBLANES)
                run(pltpu.make_async_copy(zero_ref, xs_ref.at[pl.ds(off, ZERO_ROWS)], zsem))
                return carry

            lax.fori_loop(0, full, chunk, 0)
            rem_base = base + full * ZERO_ROWS
            for b in range(SUBLANES.bit_length() - 1, ZERO_ROWS.bit_length() - 1):
                size = 1 << b

                @pl.when(((n >> b) & 1) == 1)
                def _():
                    off = pl.multiple_of(rem_base + (n & (size - 1)), SUBLANES)
                    run(pltpu.make_async_copy(
                        zero_ref.at[pl.ds(0, size)], xs_ref.at[pl.ds(off, size)], zsem))

    @pl.when(i == 0)
    def _():
        zero_ref[...] = jnp.zeros(zero_ref.shape, F32)
        fill_padding(wait=False)

    slot = lax.rem(i, 2)
    loc = loc_ref[0]
    perm = (_perm_matrix(loc[0:1, :]) | _perm_matrix(loc[1:2, :])).astype(BF16)
    grouped_ref[slot] = jnp.dot(perm, h_ref[...].astype(BF16), preferred_element_type=F32)

    def run_copy(buf):
        def make(sorted_row, local_row, size):
            return pltpu.make_async_copy(grouped_ref.at[buf, pl.ds(local_row, size)],
                                         xs_ref.at[pl.ds(sorted_row, size)], sem.at[buf])
        return make

    _run_copies(runs_ref, i, run_copy(slot), wait=False)

    @pl.when(i > 0)
    def _():
        _run_copies(runs_ref, i - 1, run_copy(1 - slot), wait=True)

    @pl.when(i == pl.num_programs(0) - 1)
    def _():
        _run_copies(runs_ref, i, run_copy(slot), wait=True)

    @pl.when(i == 0)
    def _():
        fill_padding(wait=True)


def _dispatch(tail_start, tail_len, runs_flat, loc, h2, n_rows):
    n_tok = h2.shape[0]
    return pl.pallas_call(
        _dispatch_kernel,
        grid_spec=pltpu.PrefetchScalarGridSpec(
            num_scalar_prefetch=3, grid=(n_tok // TM,),
            in_specs=[pl.BlockSpec((1, TOP_K, TM), lambda i, *_: (i, 0, 0)),
                      pl.BlockSpec((TM, D_MODEL), lambda i, *_: (i, 0))],
            out_specs=pl.BlockSpec(memory_space=pl.ANY),
            scratch_shapes=[pltpu.VMEM((ZERO_ROWS, D_MODEL), F32),
                            pltpu.VMEM((2, PERM_ROWS, D_MODEL), F32),
                            pltpu.SemaphoreType.DMA((2,)), pltpu.SemaphoreType.DMA]),
        out_shape=jax.ShapeDtypeStruct((n_rows, D_MODEL), F32),
        compiler_params=pltpu.CompilerParams(
            dimension_semantics=("arbitrary",), vmem_limit_bytes=VMEM_LIMIT),
        name="moe_dispatch",
    )(tail_start, tail_len, runs_flat, loc, h2)


def _expert_kernel(te_ref, na_ref, x_ref, w1_ref, w3_ref, w2_ref, y_ref):
    j = pl.program_id(0)

    @pl.when(j < na_ref[0])
    def _():
        h = x_ref[...].astype(BF16)
        acc = jnp.zeros((TM, D_MODEL), F32)
        for c in range(0, D_FF, FF_CHUNK):
            a = jnp.dot(h, w1_ref[0, :, c:c + FF_CHUNK], preferred_element_type=F32)
            b = jnp.dot(h, w3_ref[0, :, c:c + FF_CHUNK], preferred_element_type=F32)
            act = (a * _sigmoid(a) * b).astype(BF16)
            acc = acc + jnp.dot(act, w2_ref[0, c:c + FF_CHUNK, :], preferred_element_type=F32)
        y_ref[...] = acc

    @pl.when(j >= na_ref[0])
    def _():
        y_ref[...] = jnp.zeros(y_ref.shape, F32)


def _experts(tile_expert, n_active, xs, w1, w3, w2):
    n_tiles = xs.shape[0] // TM
    weights = lambda shape: pl.BlockSpec(shape, lambda j, te, na: (te[j], 0, 0))
    return pl.pallas_call(
        _expert_kernel,
        grid_spec=pltpu.PrefetchScalarGridSpec(
            num_scalar_prefetch=2, grid=(n_tiles,),
            in_specs=[pl.BlockSpec((TM, D_MODEL),
                                   lambda j, te, na: (jnp.minimum(j, na[0] - 1), 0)),
                      weights((1, D_MODEL, D_FF)), weights((1, D_MODEL, D_FF)),
                      weights((1, D_FF, D_MODEL))],
            out_specs=pl.BlockSpec((TM, D_MODEL), lambda j, te, na: (j, 0))),
        out_shape=jax.ShapeDtypeStruct(xs.shape, F32),
        compiler_params=pltpu.CompilerParams(
            dimension_semantics=("arbitrary",), vmem_limit_bytes=VMEM_LIMIT),
        name="moe_experts",
    )(tile_expert, n_active, xs, w1, w3, w2)


def _combine_kernel(runs_ref, loc_ref, x_ref, route_ref, y_ref, *refs, o_starts):
    stage_ref, sem = refs[-2:]
    i = pl.program_id(0)
    slot = lax.rem(i, 2)

    def run_copy(buf):
        def make(sorted_row, local_row, size):
            return pltpu.make_async_copy(y_ref.at[pl.ds(sorted_row, size)],
                                         stage_ref.at[buf, pl.ds(local_row, size)], sem.at[buf])
        return make

    @pl.when(i == 0)
    def _():
        stage_ref[...] = jnp.zeros(stage_ref.shape, F32)
        _run_copies(runs_ref, i, run_copy(slot), wait=False)

    @pl.when(i + 1 < pl.num_programs(0))
    def _():
        _run_copies(runs_ref, i + 1, run_copy(1 - slot), wait=False)

    _run_copies(runs_ref, i, run_copy(slot), wait=True)
    grouped = stage_ref[slot].astype(BF16)
    loc = loc_ref[0]
    route = route_ref[...]
    out = x_ref[...]
    for c in range(TOP_K):
        picked = lax.dot_general(_perm_matrix(loc[c:c + 1, :]).astype(BF16), grouped,
                                 (((0,), (0,)), ((), ())), preferred_element_type=F32)
        out = out + route[:, TOP_K + c:TOP_K + c + 1] * picked
    _write_parts(refs[:-2], o_starts, out)


def _combine(runs_flat, loc, x, route, y, out_rows=None):
    n_tok = x.shape[0]
    o_shapes, o_starts, o_specs = _out_parts(n_tok, out_rows)
    return pl.pallas_call(
        functools.partial(_combine_kernel, o_starts=o_starts),
        grid_spec=pltpu.PrefetchScalarGridSpec(
            num_scalar_prefetch=1, grid=(n_tok // TM,),
            in_specs=[pl.BlockSpec((1, TOP_K, TM), lambda i, *_: (i, 0, 0)),
                      pl.BlockSpec((TM, D_MODEL), lambda i, *_: (i, 0)),
                      pl.BlockSpec((TM, ROUTE_W), lambda i, *_: (i, 0)),
                      pl.BlockSpec(memory_space=pl.ANY)],
            out_specs=o_specs,
            scratch_shapes=[pltpu.VMEM((2, PERM_ROWS, D_MODEL), F32),
                            pltpu.SemaphoreType.DMA((2,))]),
        out_shape=o_shapes,
        compiler_params=pltpu.CompilerParams(
            dimension_semantics=("arbitrary",), vmem_limit_bytes=VMEM_LIMIT),
        name="moe_combine",
    )(runs_flat, loc, x, route, y)


def _moe(h2, x, route, route_t, totals, w1, w3, w2, out_rows=None):
    n_tok = x.shape[0]
    tot = totals[0, :N_EXPERTS].astype(jnp.int32)
    padded = (tot + TM - 1) // TM * TM
    end = jnp.cumsum(padded)
    start = end - padded
    max_rows = TOP_K * n_tok + (n_tok // TM) * N_EXPERTS * (SUBLANES - 1) + N_EXPERTS * (TM - 1)
    n_tiles = max_rows // TM
    n_active = end[-1] // TM
    tile = jnp.minimum(jnp.arange(n_tiles, dtype=jnp.int32), n_active - 1)
    tile_expert = jnp.sum(tile[:, None] * TM >= end[None, :], axis=1).astype(jnp.int32)
    start_b = jnp.broadcast_to(start.astype(F32)[:, None], (N_EXPERTS, LANES))
    tail_start = (start + tot).astype(jnp.int32)
    tail_len = (end.at[-1].set(n_tiles * TM) - tail_start).astype(jnp.int32)

    loc, runs = _slots(route_t, start_b)
    runs_flat = runs[:, :, :RUN_FIELDS].reshape(-1)
    xs = _dispatch(tail_start, tail_len, runs_flat, loc, h2, n_tiles * TM)
    y = _experts(tile_expert, n_active.reshape(1).astype(jnp.int32), xs, w1, w3, w2)
    return _combine(runs_flat, loc, x, route, y, out_rows)


def _tile_edges(seq_lens):
    first, last = [], []
    for n in seq_lens:
        t = n // TM
        first += [1] + [0] * (t - 1)
        last += [0] * (t - 1) + [1]
    return jnp.asarray(first, jnp.int32), jnp.asarray(last, jnp.int32)


def kernel(x_prompt, x_sample, norm1_g, w_in, b_gate, conv_w, conv_b, conv_ln_g, conv_ln_b,
           q_norm_g, k_norm_g, rpb, w_proj_a, w_proj_b, w_out, norm2_g, ffn_w1, ffn_w3, ffn_w2,
           router_w, router_b, moe_w1, moe_w3, moe_w2):
    depth = w_in.shape[0]
    seq_lens = [x_prompt.shape[1]] * x_prompt.shape[0] + [x_sample.shape[1]] * x_sample.shape[0]
    assert all(n % TM == 0 and n // GRID_W >= 2 * TILE_ROWS for n in seq_lens)
    first, last = _tile_edges(seq_lens)
    x_parts = [x_prompt.reshape(-1, D_MODEL), x_sample.reshape(-1, D_MODEL)]
    out_rows = [p.shape[0] for p in x_parts]

    head_id = jnp.arange(D_ATTN) // HEAD_DIM
    head_mean = jnp.where(head_id[:, None] == head_id[None, :], 1.0 / HEAD_DIM, 0.0).astype(BF16)
    row = lambda a: a.reshape(1, -1).astype(F32)

    moe_bf16 = {}
    for l in range(depth):
        final = out_rows if l == depth - 1 else None
        j = l // 2
        moe = l % 2 == 1
        feeds_moe = not moe and l + 1 < depth
        table = _bias_table(rpb[l])
        side_w = moe_w2[j] if moe else (moe_w1[j] if feeds_moe else None)
        (vglu, q, k, v, ga, gb), side = _in_proj(
            x_parts, row(norm1_g[l]), _to_bf16(w_in, l), row(b_gate[l]),
            row(jnp.tile(q_norm_g[l], N_HEADS)), row(jnp.tile(k_norm_g[l], N_HEADS)), head_mean,
            side_w=side_w)
        if side is not None:
            moe_bf16["w2" if moe else "w1", j] = side
        if moe:
            rw = jnp.pad(router_w[j], ((0, 0), (0, LANES - N_EXPERTS))).astype(BF16)
            rb = jnp.pad(row(router_b[j]), ((0, 0), (0, LANES - N_EXPERTS)))
        else:
            rw = jnp.zeros((D_MODEL, LANES), BF16)
            rb = jnp.zeros((1, LANES), F32)
        cw = jnp.pad(conv_w[l], ((0, 1), (0, 0)))
        outs = _mixer(
            x_parts, vglu, q, k, v, ga, gb, first, last, cw, row(conv_b[l]), row(conv_ln_g[l]),
            row(conv_ln_b[l]), table, _to_bf16(w_proj_a, l), _to_bf16(w_proj_b, l),
            _to_bf16(w_out, l), row(norm2_g[l]), rw, rb, with_router=moe)
        if moe:
            x, h2, route, route_t, totals = outs
            w1 = moe_bf16.get(("w1", j))
            w3 = moe_bf16.get(("w3", j))
            res = _moe(h2, x, route, route_t, totals,
                       _to_bf16(moe_w1[j]) if w1 is None else w1,
                       _to_bf16(moe_w3[j]) if w3 is None else w3,
                       moe_bf16["w2", j], final)
        else:
            x, h2 = outs
            res, side = _ffn(h2, x, _to_bf16(ffn_w1, j), _to_bf16(ffn_w3, j),
                             _to_bf16(ffn_w2, j), final,
                             side_w=moe_w3[j] if feeds_moe else None)
            if side is not None:
                moe_bf16["w3", j] = side
        x_parts = list(res)

    return (x_parts[0].reshape(x_prompt.shape), x_parts[1].reshape(x_sample.shape))
```

```python
import functools

import jax
import jax.numpy as jnp
from jax import lax
from jax.experimental import pallas as pl
from jax.experimental.pallas import tpu as pltpu

D_MODEL = 1024
GRID_W = 64
D_CONV = 512
CONV_W = 31
N_HEADS = 8
HEAD_DIM = 64
D_ATTN = N_HEADS * HEAD_DIM
MAX_WIN_H = 8
WIN_W = 16
D_FF = 2816
N_EXPERTS = 8
TOP_K = 2
EPS = 1e-6

LANES = 128
SUBLANES = 8
BF16_ROWS = 16
TM = 512
TILE_ROWS = TM // GRID_W
KV_HALO = 256
CONV_HALO = 16
CONV_PAD = CONV_W // 2
CONV_ROWS = 256
N_RO = 2 * MAX_WIN_H - 1
N_CO = 2 * WIN_W - 1
HEAD_GROUP = 4
ROUTE_W = 8
ATT_UNROLL = 8
NEG_BIG = -1e30
LOG2E = 1.4426950408889634
FF_CHUNK = 256
VMEM_LIMIT = 56 * 1024 * 1024

F32 = jnp.float32
BF16 = jnp.bfloat16


def _sigmoid(x):
    return 1.0 / (1.0 + jnp.exp(-x))


def _round_up_rows(count):
    return jnp.floor((count + (SUBLANES - 1)) * (1.0 / SUBLANES)) * SUBLANES


def _part_specs(parts, width=D_MODEL):
    starts, specs, tile0 = [], [], 0
    for a in parts:
        n = a.shape[0] // TM
        specs.append(pl.BlockSpec(
            (TM, width), lambda i, *_, s=tile0, n=n: (jnp.clip(i - s, 0, n - 1), 0)))
        starts.append(tile0)
        tile0 += n
    return tuple(starts), specs, tile0


def _read_parts(refs, starts):
    i = pl.program_id(0)
    x = refs[0][...]
    for ref, s in zip(refs[1:], starts[1:]):
        x = jnp.where(i >= s, ref[...], x)
    return x


def _write_parts(refs, starts, value):
    i = pl.program_id(0)
    ends = starts[1:] + (None,)
    for ref, s, e in zip(refs, starts, ends):
        owns = i >= s if e is None else (i >= s) & (i < e)

        @pl.when(owns)
        def _():
            ref[...] = value


CAST_ROWS = 512
CAST_BLOCK_ELEMS = 3 * 1024 * 1024


def _cast_kernel(w_ref, o_ref):
    o_ref[...] = w_ref[...].astype(BF16)


def _to_bf16(w, take=None):
    lead = w.shape[:-2]
    k, n = w.shape[-2:]
    w3 = w.reshape((-1, k, n))
    n_mat = w3.shape[0] if take is None else 1
    first = 0 if take is None else take
    rows = k if k * n <= CAST_BLOCK_ELEMS else next(
        (r for r in (CAST_ROWS, CAST_ROWS // 2, CAST_ROWS // 4) if k % r == 0), k)
    out = pl.pallas_call(
        _cast_kernel,
        grid=(n_mat, k // rows),
        in_specs=[pl.BlockSpec((1, rows, n), lambda e, r: (first + e, r, 0))],
        out_specs=pl.BlockSpec((1, rows, n), lambda e, r: (e, r, 0)),
        out_shape=jax.ShapeDtypeStruct((n_mat, k, n), BF16),
        compiler_params=pltpu.CompilerParams(
            dimension_semantics=("arbitrary", "arbitrary"), vmem_limit_bytes=VMEM_LIMIT),
        name="to_bf16",
    )(w3)
    return out.reshape((k, n) if take is not None else lead + (k, n))


def _side_cast(w, n_steps):
    w2 = w.reshape(-1, w.shape[-1])
    rows = w2.shape[0] // n_steps
    if w2.shape[0] % n_steps or rows % BF16_ROWS:
        return None
    spec = pl.BlockSpec((rows, w2.shape[1]), lambda i, *_: (i, 0))
    return w2, spec, jax.ShapeDtypeStruct(w2.shape, BF16)


def _bias_table_kernel(rpb_ref, out_ref):
    ro = pl.program_id(0)
    shape = (GRID_W, HEAD_GROUP * GRID_W)
    kcol = lax.broadcasted_iota(jnp.int32, shape, 0)
    lane = lax.broadcasted_iota(jnp.int32, shape, 1)
    head_local = lane // GRID_W
    qcol = lane - head_local * GRID_W
    start_c = jnp.clip(qcol - WIN_W // 2, 0, GRID_W - WIN_W)
    valid = (kcol >= start_c) & (kcol < start_c + WIN_W)
    co = jnp.clip(kcol - qcol + (WIN_W - 1), 0, 2 * WIN_W - 2)
    for g in range(N_HEADS // HEAD_GROUP):
        acc = jnp.zeros(shape, F32)
        for d in range(N_CO):
            val = jnp.zeros(shape, F32)
            for hl in range(HEAD_GROUP):
                h = g * HEAD_GROUP + hl
                val = jnp.where(head_local == hl, rpb_ref[(h * N_RO + ro) * N_CO + d], val)
            acc = jnp.where(co == d, val, acc)
        out_ref[0, g] = jnp.where(valid, acc * LOG2E, NEG_BIG)


def _bias_table(rpb_l):
    n_groups = N_HEADS // HEAD_GROUP
    width = HEAD_GROUP * GRID_W
    return pl.pallas_call(
        _bias_table_kernel,
        grid=(N_RO,),
        in_specs=[pl.BlockSpec(memory_space=pltpu.SMEM)],
        out_specs=pl.BlockSpec((1, n_groups, GRID_W, width), lambda r: (r, 0, 0, 0)),
        out_shape=jax.ShapeDtypeStruct((N_RO, n_groups, GRID_W, width), F32),
        name="bias_table",
    )(rpb_l.reshape(-1))


def _in_proj_kernel(*refs, x_starts, side):
    n_parts = len(x_starts)
    g_ref, w_ref, bg_ref, qg_ref, kg_ref, hm_ref = refs[n_parts:n_parts + 6]
    outs = refs[n_parts + 6 + side:]
    vglu_ref, q_ref, k_ref, v_ref, ga_ref, gb_ref = outs[:6]
    if side:
        outs[6][...] = refs[n_parts + 6][...].astype(BF16)
    x = _read_parts(refs[:n_parts], x_starts)
    ms = jnp.mean(x * x, axis=-1, keepdims=True)
    h = (x * lax.rsqrt(ms + EPS) * g_ref[...]).astype(BF16)

    def proj(lo, hi):
        return jnp.dot(h, w_ref[:, lo:hi], preferred_element_type=F32)

    def head_norm(t, gain):
        msq = jnp.dot((t * t).astype(BF16), hm_ref[...], preferred_element_type=F32)
        return t * lax.rsqrt(msq + EPS) * gain

    a = proj(0, D_CONV)
    gate = proj(D_CONV, 2 * D_CONV)
    vglu_ref[...] = (a * _sigmoid(gate)).astype(BF16)
    o = 2 * D_CONV
    q_ref[...] = (head_norm(proj(o, o + D_ATTN), qg_ref[...])
                  * (HEAD_DIM ** -0.5 * LOG2E)).astype(BF16)
    k_ref[...] = head_norm(proj(o + D_ATTN, o + 2 * D_ATTN), kg_ref[...]).astype(BF16)
    v_ref[...] = proj(o + 2 * D_ATTN, o + 3 * D_ATTN).astype(BF16)
    o += 3 * D_ATTN
    bg = bg_ref[...]
    ga_ref[...] = _sigmoid(proj(o, o + D_MODEL) + bg[:, :D_MODEL]).astype(BF16)
    gb_ref[...] = _sigmoid(proj(o + D_MODEL, o + 2 * D_MODEL) + bg[:, D_MODEL:]).astype(BF16)


def _in_proj(x_parts, norm_g, w_in, b_gate, q_norm_g, k_norm_g, head_mean, side_w=None):
    x_starts, x_specs, n_tiles = _part_specs(x_parts)
    n_tok = n_tiles * TM
    d_in = w_in.shape[1]
    full = lambda shape: pl.BlockSpec(shape, lambda i: (0, 0))
    tile = lambda width: pl.BlockSpec((TM, width), lambda i: (i, 0))
    out = lambda width: jax.ShapeDtypeStruct((n_tok, width), BF16)
    plan = _side_cast(side_w, n_tiles) if side_w is not None else None
    side_args, side_in, side_out, side_shape = [], [], [], []
    if plan is not None:
        side_args, side_in, side_out, side_shape = [plan[0]], [plan[1]], [plan[1]], [plan[2]]
    res = pl.pallas_call(
        functools.partial(_in_proj_kernel, x_starts=x_starts, side=len(side_args)),
        grid=(n_tiles,),
        in_specs=x_specs + [full((1, D_MODEL)), full((D_MODEL, d_in)),
                            full((1, 2 * D_MODEL)), full((1, D_ATTN)), full((1, D_ATTN)),
                            full((D_ATTN, D_ATTN))] + side_in,
        out_specs=[tile(D_CONV), tile(D_ATTN), tile(D_ATTN), tile(D_ATTN),
                   tile(D_MODEL), tile(D_MODEL)] + side_out,
        out_shape=[out(D_CONV), out(D_ATTN), out(D_ATTN), out(D_ATTN),
                   out(D_MODEL), out(D_MODEL)] + side_shape,
        compiler_params=pltpu.CompilerParams(
            dimension_semantics=("arbitrary",), vmem_limit_bytes=VMEM_LIMIT),
        name="in_proj",
    )(*x_parts, norm_g, w_in, b_gate, q_norm_g, k_norm_g, head_mean, *side_args)
    side_bf16 = None
    if side_w is not None:
        side_bf16 = res[6].reshape(side_w.shape) if plan is not None else _to_bf16(side_w)
    return res[:6], side_bf16


def _mixer_kernel(first_ref, last_ref, *refs, x_starts, with_router):
    n_parts = len(x_starts)
    (vg_ref, vgp_ref, vgn_ref, q_ref,
     k_ref, kp_ref, kn_ref, v_ref, vp_ref, vn_ref,
     ga_ref, gb_ref, cw_ref, cb_ref, lg_ref, lb_ref, tab_ref,
     wa_ref, wb_ref, wo_ref, n2_ref, rw_ref, rb_ref) = refs[n_parts:n_parts + 23]
    rest = refs[n_parts + 23:]
    if with_router:
        xo_ref, h2_ref, route_ref, loc_ref, runs_ref, tot_ref = rest[:6]
    else:
        xo_ref, h2_ref = rest[:2]
    ext_ref, sh_ref, conv_ref, kext_ref, vext_ref, att_ref = rest[-6:]
    i = pl.program_id(0)
    is_first = first_ref[i]
    is_last = last_ref[i]

    keep_prev = (1 - is_first).astype(F32)
    keep_next = (1 - is_last).astype(F32)
    ext_ref[0:CONV_HALO, :] = vgp_ref[...].astype(F32) * keep_prev
    ext_ref[CONV_HALO:CONV_HALO + TM, :] = vg_ref[...].astype(F32)
    ext_ref[CONV_HALO + TM:, :] = vgn_ref[...].astype(F32) * keep_next
    n_sh_rows = sh_ref.shape[1]
    row_chunk = CONV_ROWS
    for cb in range(D_CONV // LANES):
        cols = slice(cb * LANES, (cb + 1) * LANES)
        for b in range(SUBLANES):
            sh_ref[b] = ext_ref[b:b + n_sh_rows, cols]

        def conv_rows(rc, carry, cols=cols):
            base = pl.multiple_of(rc * row_chunk, row_chunk)
            acc = jnp.zeros((row_chunk, LANES), F32)
            for j in range(CONV_W):
                off = CONV_HALO - CONV_PAD + j
                a8, b = off // SUBLANES, off % SUBLANES
                acc = acc + (sh_ref[b, pl.ds(base + SUBLANES * a8, row_chunk), :]
                             * cw_ref[j:j + 1, cols])
            conv_ref[pl.ds(base, row_chunk), cols] = acc
            return carry

        lax.fori_loop(0, TM // row_chunk, conv_rows, 0)
    y = conv_ref[...] + cb_ref[...]
    mu = jnp.mean(y, axis=-1, keepdims=True)
    yc = y - mu
    var = jnp.mean(yc * yc, axis=-1, keepdims=True)
    yn = yc * lax.rsqrt(var + EPS) * lg_ref[...] + lb_ref[...]
    act = (yn * _sigmoid(yn)).astype(BF16)
    br_a = jnp.dot(act, wa_ref[...], preferred_element_type=F32)

    kext_ref[0:KV_HALO, :] = kp_ref[...]
    kext_ref[KV_HALO:KV_HALO + TM, :] = k_ref[...]
    kext_ref[KV_HALO + TM:, :] = kn_ref[...]
    vext_ref[0:KV_HALO, :] = vp_ref[...]
    vext_ref[KV_HALO:KV_HALO + TM, :] = v_ref[...]
    vext_ref[KV_HALO + TM:, :] = vn_ref[...]
    gw = HEAD_GROUP * HEAD_DIM
    lane_head = lax.broadcasted_iota(jnp.int32, (GRID_W, gw), 1) // HEAD_DIM
    win = MAX_WIN_H * GRID_W

    def row_body(j, carry):
        ws = j - MAX_WIN_H // 2
        ws = jnp.where(is_first == 1, jnp.maximum(ws, 0), ws)
        ws = jnp.where(is_last == 1, jnp.minimum(ws, 0), ws)
        ro0 = ws - j + (MAX_WIN_H - 1)
        kstart = pl.multiple_of((ws + KV_HALO // GRID_W) * GRID_W, GRID_W)
        qstart = pl.multiple_of(j * GRID_W, GRID_W)
        for g in range(N_HEADS // HEAD_GROUP):
            cols = slice(g * gw, (g + 1) * gw)
            q4 = q_ref[pl.ds(qstart, GRID_W), cols]
            qexp = jnp.concatenate(
                [jnp.where(lane_head == hl, q4, jnp.zeros_like(q4)) for hl in range(HEAD_GROUP)],
                axis=0)
            kslab = kext_ref[pl.ds(kstart, win), cols]
            st = lax.dot_general(kslab, qexp, (((1,), (1,)), ((), ())),
                                 preferred_element_type=F32)
            st = st + jnp.concatenate([tab_ref[ro0 + w, g] for w in range(MAX_WIN_H)], axis=0)
            m = jnp.max(st, axis=0, keepdims=True)
            pexp = jnp.exp2(st - m)
            inv = 1.0 / jnp.sum(pexp, axis=0, keepdims=True)
            pn = (pexp * inv).astype(BF16)
            vslab = vext_ref[pl.ds(kstart, win), cols]
            o4 = lax.dot_general(pn, vslab, (((0,), (0,)), ((), ())),
                                 preferred_element_type=F32)
            out = o4[0:GRID_W]
            for hl in range(1, HEAD_GROUP):
                out = jnp.where(lane_head == hl, o4[hl * GRID_W:(hl + 1) * GRID_W], out)
            att_ref[pl.ds(qstart, GRID_W), cols] = out
        return carry

    lax.fori_loop(0, TILE_ROWS, row_body, 0, unroll=ATT_UNROLL)
    br_b = jnp.dot(att_ref[...].astype(BF16), wb_ref[...], preferred_element_type=F32)

    merged = ga_ref[...].astype(F32) * br_a + gb_ref[...].astype(F32) * br_b
    x_new = (_read_parts(refs[:n_parts], x_starts)
             + jnp.dot(merged.astype(BF16), wo_ref[...], preferred_element_type=F32))
    xo_ref[...] = x_new
    ms = jnp.mean(x_new * x_new, axis=-1, keepdims=True)
    h2 = x_new * lax.rsqrt(ms + EPS) * n2_ref[...]
    h2_ref[...] = h2.astype(h2_ref.dtype)

    if with_router:
        logits = jnp.dot(h2.astype(BF16), rw_ref[...], preferred_element_type=F32) + rb_ref[...]
        elane = lax.broadcasted_iota(jnp.int32, logits.shape, 1).astype(F32)
        logits = jnp.where(elane < N_EXPERTS, logits, -jnp.inf)
        m1 = jnp.max(logits, axis=1, keepdims=True)
        i1 = jnp.min(jnp.where(logits == m1, elane, float(LANES)), axis=1, keepdims=True)
        others = jnp.where(elane == i1, -jnp.inf, logits)
        m2 = jnp.max(others, axis=1, keepdims=True)
        i2 = jnp.min(jnp.where(others == m2, elane, float(LANES)), axis=1, keepdims=True)
        e2 = jnp.exp(m2 - m1)
        w1 = 1.0 / (1.0 + e2)
        w2 = e2 / (1.0 + e2)
        route = jnp.where(elane == 0, i1, jnp.where(elane == 1, i2, jnp.where(
            elane == 2, w1, jnp.where(elane == 3, w2, 0.0))))
        route_ref[...] = route[:, :ROUTE_W]

        @pl.when(i == 0)
        def _():
            tot_ref[...] = jnp.zeros(tot_ref.shape, F32)

        rt = route.T[:ROUTE_W, :]
        eidx = lax.broadcasted_iota(jnp.int32, (N_EXPERTS, TM), 0).astype(F32)
        e1 = eidx == rt[0:1, :]
        e2 = eidx == rt[1:2, :]
        chosen = (e1 | e2).astype(F32)
        before = lax.broadcasted_iota(jnp.int32, (TM, TM), 0) < lax.broadcasted_iota(
            jnp.int32, (TM, TM), 1)
        rank = jnp.dot(chosen.astype(BF16), before.astype(BF16), preferred_element_type=F32)
        rows_b = jnp.broadcast_to(_round_up_rows(jnp.sum(chosen, axis=1, keepdims=True)),
                                  (N_EXPERTS, LANES))
        expert = lax.broadcasted_iota(jnp.int32, (N_EXPERTS, LANES), 0)
        incl = rows_b
        for step in (1, 2, 4):
            incl = incl + jnp.where(expert >= step, pltpu.roll(incl, step, axis=0), 0.0)
        local = incl - rows_b
        pos = rank + local[:, 0:1]
        l1 = jnp.sum(jnp.where(e1, pos, 0.0), axis=0, keepdims=True)
        l2 = jnp.sum(jnp.where(e2, pos, 0.0), axis=0, keepdims=True)
        loc_ref[0] = jnp.concatenate([l1, l2], axis=0).astype(jnp.int32)
        lane = lax.broadcasted_iota(jnp.int32, (N_EXPERTS, LANES), 1)
        runs_ref[0] = jnp.where(lane == 0, tot_ref[...], jnp.where(
            lane == 1, rows_b, local)).astype(jnp.int32)
        tot_ref[...] += rows_b


def _mixer(x_parts, vglu, q, k, v, ga, gb, first, last, conv_w, conv_b, ln_g, ln_b, table,
           w_a, w_b, w_o, norm2_g, router_w, router_b, *, with_router):
    x_starts, x_specs, n_tiles = _part_specs(x_parts)
    n_tok = n_tiles * TM
    conv_blocks = TM // CONV_HALO
    kv_blocks = TM // KV_HALO
    full2 = lambda shape: pl.BlockSpec(shape, lambda i, f, l: (0, 0))
    tile = lambda width: pl.BlockSpec((TM, width), lambda i, f, l: (i, 0))
    in_specs = x_specs + [
        tile(D_CONV),
        pl.BlockSpec((CONV_HALO, D_CONV), lambda i, f, l: (i * conv_blocks - 1 + f[i], 0)),
        pl.BlockSpec((CONV_HALO, D_CONV), lambda i, f, l: ((i + 1) * conv_blocks - l[i], 0)),
        tile(D_ATTN),
        tile(D_ATTN),
        pl.BlockSpec((KV_HALO, D_ATTN), lambda i, f, l: (i * kv_blocks - 1 + f[i], 0)),
        pl.BlockSpec((KV_HALO, D_ATTN), lambda i, f, l: ((i + 1) * kv_blocks - l[i], 0)),
        tile(D_ATTN),
        pl.BlockSpec((KV_HALO, D_ATTN), lambda i, f, l: (i * kv_blocks - 1 + f[i], 0)),
        pl.BlockSpec((KV_HALO, D_ATTN), lambda i, f, l: ((i + 1) * kv_blocks - l[i], 0)),
        tile(D_MODEL),
        tile(D_MODEL),
        full2((CONV_W + 1, D_CONV)),
        full2((1, D_CONV)), full2((1, D_CONV)), full2((1, D_CONV)),
        pl.BlockSpec(table.shape, lambda i, f, l: (0, 0, 0, 0)),
        full2((D_CONV, D_MODEL)), full2((D_ATTN, D_MODEL)), full2((D_MODEL, D_MODEL)),
        full2((1, D_MODEL)),
        full2((D_MODEL, LANES)), full2((1, LANES)),
    ]
    out_specs = [tile(D_MODEL), tile(D_MODEL)]
    out_shape = [jax.ShapeDtypeStruct((n_tok, D_MODEL), F32),
                 jax.ShapeDtypeStruct((n_tok, D_MODEL), F32 if with_router else BF16)]
    if with_router:
        out_specs += [tile(ROUTE_W),
                      pl.BlockSpec((1, TOP_K, TM), lambda i, f, l: (i, 0, 0)),
                      pl.BlockSpec((1, N_EXPERTS, LANES), lambda i, f, l: (i, 0, 0)),
                      pl.BlockSpec((N_EXPERTS, LANES), lambda i, f, l: (0, 0))]
        out_shape += [jax.ShapeDtypeStruct((n_tok, ROUTE_W), F32),
                      jax.ShapeDtypeStruct((n_tiles, TOP_K, TM), jnp.int32),
                      jax.ShapeDtypeStruct((n_tiles, N_EXPERTS, LANES), jnp.int32),
                      jax.ShapeDtypeStruct((N_EXPERTS, LANES), F32)]
    ext_rows = TM + 2 * CONV_HALO
    scratch = [
        pltpu.VMEM((ext_rows, D_CONV), F32),
        pltpu.VMEM((SUBLANES, ext_rows - SUBLANES, LANES), F32),
        pltpu.VMEM((TM, D_CONV), F32),
        pltpu.VMEM((TM + 2 * KV_HALO, D_ATTN), BF16),
        pltpu.VMEM((TM + 2 * KV_HALO, D_ATTN), BF16),
        pltpu.VMEM((TM, D_ATTN), F32),
    ]
    return pl.pallas_call(
        functools.partial(_mixer_kernel, x_starts=x_starts, with_router=with_router),
        grid_spec=pltpu.PrefetchScalarGridSpec(
            num_scalar_prefetch=2, grid=(n_tiles,),
            in_specs=in_specs, out_specs=out_specs, scratch_shapes=scratch),
        out_shape=out_shape,
        compiler_params=pltpu.CompilerParams(
            dimension_semantics=("arbitrary",), vmem_limit_bytes=VMEM_LIMIT),
        name="mixer",
    )(first, last, *x_parts, vglu, vglu, vglu, q, k, k, k, v, v, v, ga, gb,
      conv_w, conv_b, ln_g, ln_b, table, w_a, w_b, w_o, norm2_g, router_w, router_b)


def _ffn_kernel(h_ref, x_ref, w1_ref, w3_ref, w2_ref, *rest, o_starts, side):
    o_refs = rest
    if side:
        rest[-1][...] = rest[0][...].astype(BF16)
        o_refs = rest[1:-1]
    h = h_ref[...]
    acc = x_ref[...]
    for c in range(0, D_FF, FF_CHUNK):
        a = jnp.dot(h, w1_ref[:, c:c + FF_CHUNK], preferred_element_type=F32)
        b = jnp.dot(h, w3_ref[:, c:c + FF_CHUNK], preferred_element_type=F32)
        act = (a * _sigmoid(a) * b).astype(BF16)
        acc = acc + jnp.dot(act, w2_ref[c:c + FF_CHUNK, :], preferred_element_type=F32)
    _write_parts(o_refs, o_starts, acc)


def _out_parts(n_tok, out_rows):
    shapes = [jax.ShapeDtypeStruct((r, D_MODEL), F32) for r in (out_rows or [n_tok])]
    starts, specs, _ = _part_specs(shapes)
    return shapes, starts, specs


def _ffn(h2, x, w1, w3, w2, out_rows=None, side_w=None):
    n_tok = x.shape[0]
    tile = lambda: pl.BlockSpec((TM, D_MODEL), lambda i: (i, 0))
    single = dict(pipeline_mode=pl.Buffered(1))
    o_shapes, o_starts, o_specs = _out_parts(n_tok, out_rows)
    plan = _side_cast(side_w, n_tok // TM) if side_w is not None else None
    side_args, side_spec, side_shape = [], [], []
    if plan is not None:
        side_args, side_spec, side_shape = [plan[0]], [plan[1]], [plan[2]]
    res = pl.pallas_call(
        functools.partial(_ffn_kernel, o_starts=o_starts, side=len(side_args)),
        grid=(n_tok // TM,),
        in_specs=[tile(), tile(),
                  pl.BlockSpec((D_MODEL, D_FF), lambda i: (0, 0), **single),
                  pl.BlockSpec((D_MODEL, D_FF), lambda i: (0, 0), **single),
                  pl.BlockSpec((D_FF, D_MODEL), lambda i: (0, 0), **single)] + side_spec,
        out_specs=o_specs + side_spec,
        out_shape=o_shapes + side_shape,
        compiler_params=pltpu.CompilerParams(
            dimension_semantics=("arbitrary",), vmem_limit_bytes=VMEM_LIMIT),
        name="ffn",
    )(h2, x, w1, w3, w2, *side_args)
    side_bf16 = None
    if side_w is not None:
        side_bf16 = res[-1].reshape(side_w.shape) if plan is not None else _to_bf16(side_w)
    return res[:len(o_shapes)], side_bf16


RUN_FIELDS = 3
PERM_ROWS = TOP_K * TM + N_EXPERTS * SUBLANES


def _run_copies(runs_ref, tile, make_copy, wait):
    for e in range(N_EXPERTS):
        base = (tile * N_EXPERTS + e) * RUN_FIELDS
        sorted_row, n, local_row = runs_ref[base], runs_ref[base + 1], runs_ref[base + 2]

        @pl.when(n > 0)
        def _():
            copy = make_copy(pl.multiple_of(sorted_row, SUBLANES),
                             pl.multiple_of(local_row, SUBLANES), pl.multiple_of(n, SUBLANES))
            copy.wait() if wait else copy.start()


def _perm_matrix(loc_row):
    rows = lax.broadcasted_iota(jnp.int32, (PERM_ROWS, TM), 0)
    return rows == loc_row


ZERO_ROWS = 64


def _dispatch_kernel(ts_ref, tl_ref, runs_ref, loc_ref, h_ref, xs_ref,
                     zero_ref, grouped_ref, sem, zsem):
    i = pl.program_id(0)

    def fill_padding(wait):
        def run(copy):
            copy.wait() if wait else copy.start()

        for e in range(N_EXPERTS):
            base = ts_ref[e]
            n = tl_ref[e]
            full = n // ZERO_ROWS

            def chunk(c, carry):
                off = pl.multiple_of(base + c * ZERO_ROWS, SUBLANES)
                run(pltpu.make_async_copy(zero_ref, xs_ref.at[pl.ds(off, ZERO_ROWS)], zsem))
                return carry

            lax.fori_loop(0, full, chunk, 0)
            rem_base = base + full * ZERO_ROWS
            for b in range(SUBLANES.bit_length() - 1, ZERO_ROWS.bit_length() - 1):
                size = 1 << b

                @pl.when(((n >> b) & 1) == 1)
                def _():
                    off = pl.multiple_of(rem_base + (n & (size - 1)), SUBLANES)
                    run(pltpu.make_async_copy(
                        zero_ref.at[pl.ds(0, size)], xs_ref.at[pl.ds(off, size)], zsem))

    @pl.when(i == 0)
    def _():
        zero_ref[...] = jnp.zeros(zero_ref.shape, F32)
        fill_padding(wait=False)

    slot = lax.rem(i, 2)
    loc = loc_ref[0]
    perm = (_perm_matrix(loc[0:1, :]) | _perm_matrix(loc[1:2, :])).astype(BF16)
    grouped_ref[slot] = jnp.dot(perm, h_ref[...].astype(BF16), preferred_element_type=F32)

    def run_copy(buf):
        def make(sorted_row, local_row, size):
            return pltpu.make_async_copy(grouped_ref.at[buf, pl.ds(local_row, size)],
                                         xs_ref.at[pl.ds(sorted_row, size)], sem.at[buf])
        return make

    _run_copies(runs_ref, i, run_copy(slot), wait=False)

    @pl.when(i > 0)
    def _():
        _run_copies(runs_ref, i - 1, run_copy(1 - slot), wait=True)

    @pl.when(i == pl.num_programs(0) - 1)
    def _():
        _run_copies(runs_ref, i, run_copy(slot), wait=True)

    @pl.when(i == 0)
    def _():
        fill_padding(wait=True)


def _dispatch(tail_start, tail_len, runs_flat, loc, h2, n_rows):
    n_tok = h2.shape[0]
    return pl.pallas_call(
        _dispatch_kernel,
        grid_spec=pltpu.PrefetchScalarGridSpec(
            num_scalar_prefetch=3, grid=(n_tok // TM,),
            in_specs=[pl.BlockSpec((1, TOP_K, TM), lambda i, *_: (i, 0, 0)),
                      pl.BlockSpec((TM, D_MODEL), lambda i, *_: (i, 0))],
            out_specs=pl.BlockSpec(memory_space=pl.ANY),
            scratch_shapes=[pltpu.VMEM((ZERO_ROWS, D_MODEL), F32),
                            pltpu.VMEM((2, PERM_ROWS, D_MODEL), F32),
                            pltpu.SemaphoreType.DMA((2,)), pltpu.SemaphoreType.DMA]),
        out_shape=jax.ShapeDtypeStruct((n_rows, D_MODEL), F32),
        compiler_params=pltpu.CompilerParams(
            dimension_semantics=("arbitrary",), vmem_limit_bytes=VMEM_LIMIT),
        name="moe_dispatch",
    )(tail_start, tail_len, runs_flat, loc, h2)


def _expert_kernel(te_ref, na_ref, x_ref, w1_ref, w3_ref, w2_ref, y_ref):
    j = pl.program_id(0)

    @pl.when(j < na_ref[0])
    def _():
        h = x_ref[...].astype(BF16)
        acc = jnp.zeros((TM, D_MODEL), F32)
        for c in range(0, D_FF, FF_CHUNK):
            a = jnp.dot(h, w1_ref[0, :, c:c + FF_CHUNK], preferred_element_type=F32)
            b = jnp.dot(h, w3_ref[0, :, c:c + FF_CHUNK], preferred_element_type=F32)
            act = (a * _sigmoid(a) * b).astype(BF16)
            acc = acc + jnp.dot(act, w2_ref[0, c:c + FF_CHUNK, :], preferred_element_type=F32)
        y_ref[...] = acc

    @pl.when(j >= na_ref[0])
    def _():
        y_ref[...] = jnp.zeros(y_ref.shape, F32)


def _experts(tile_expert, n_active, xs, w1, w3, w2):
    n_tiles = xs.shape[0] // TM
    weights = lambda shape: pl.BlockSpec(shape, lambda j, te, na: (te[j], 0, 0))
    return pl.pallas_call(
        _expert_kernel,
        grid_spec=pltpu.PrefetchScalarGridSpec(
            num_scalar_prefetch=2, grid=(n_tiles,),
            in_specs=[pl.BlockSpec((TM, D_MODEL),
                                   lambda j, te, na: (jnp.minimum(j, na[0] - 1), 0)),
                      weights((1, D_MODEL, D_FF)), weights((1, D_MODEL, D_FF)),
                      weights((1, D_FF, D_MODEL))],
            out_specs=pl.BlockSpec((TM, D_MODEL), lambda j, te, na: (j, 0))),
        out_shape=jax.ShapeDtypeStruct(xs.shape, F32),
        compiler_params=pltpu.CompilerParams(
            dimension_semantics=("arbitrary",), vmem_limit_bytes=VMEM_LIMIT),
        name="moe_experts",
    )(tile_expert, n_active, xs, w1, w3, w2)


def _combine_kernel(runs_ref, loc_ref, x_ref, route_ref, y_ref, *refs, o_starts):
    stage_ref, sem = refs[-2:]
    i = pl.program_id(0)
    slot = lax.rem(i, 2)

    def run_copy(buf):
        def make(sorted_row, local_row, size):
            return pltpu.make_async_copy(y_ref.at[pl.ds(sorted_row, size)],
                                         stage_ref.at[buf, pl.ds(local_row, size)], sem.at[buf])
        return make

    @pl.when(i == 0)
    def _():
        stage_ref[...] = jnp.zeros(stage_ref.shape, F32)
        _run_copies(runs_ref, i, run_copy(slot), wait=False)

    @pl.when(i + 1 < pl.num_programs(0))
    def _():
        _run_copies(runs_ref, i + 1, run_copy(1 - slot), wait=False)

    _run_copies(runs_ref, i, run_copy(slot), wait=True)
    grouped = stage_ref[slot].astype(BF16)
    loc = loc_ref[0]
    route = route_ref[...]
    out = x_ref[...]
    for c in range(TOP_K):
        picked = lax.dot_general(_perm_matrix(loc[c:c + 1, :]).astype(BF16), grouped,
                                 (((0,), (0,)), ((), ())), preferred_element_type=F32)
        out = out + route[:, TOP_K + c:TOP_K + c + 1] * picked
    _write_parts(refs[:-2], o_starts, out)


def _combine(runs_flat, loc, x, route, y, out_rows=None):
    n_tok = x.shape[0]
    o_shapes, o_starts, o_specs = _out_parts(n_tok, out_rows)
    return pl.pallas_call(
        functools.partial(_combine_kernel, o_starts=o_starts),
        grid_spec=pltpu.PrefetchScalarGridSpec(
            num_scalar_prefetch=1, grid=(n_tok // TM,),
            in_specs=[pl.BlockSpec((1, TOP_K, TM), lambda i, *_: (i, 0, 0)),
                      pl.BlockSpec((TM, D_MODEL), lambda i, *_: (i, 0)),
                      pl.BlockSpec((TM, ROUTE_W), lambda i, *_: (i, 0)),
                      pl.BlockSpec(memory_space=pl.ANY)],
            out_specs=o_specs,
            scratch_shapes=[pltpu.VMEM((2, PERM_ROWS, D_MODEL), F32),
                            pltpu.SemaphoreType.DMA((2,))]),
        out_shape=o_shapes,
        compiler_params=pltpu.CompilerParams(
            dimension_semantics=("arbitrary",), vmem_limit_bytes=VMEM_LIMIT),
        name="moe_combine",
    )(runs_flat, loc, x, route, y)


def _moe(h2, x, route, loc, runs, totals, w1, w3, w2, out_rows=None):
    n_tok = x.shape[0]
    tot = totals[:, 0].astype(jnp.int32)
    padded = (tot + TM - 1) // TM * TM
    end = jnp.cumsum(padded)
    start = end - padded
    max_rows = TOP_K * n_tok + (n_tok // TM) * N_EXPERTS * (SUBLANES - 1) + N_EXPERTS * (TM - 1)
    n_tiles = max_rows // TM
    n_active = end[-1] // TM
    tile = jnp.minimum(jnp.arange(n_tiles, dtype=jnp.int32), n_active - 1)
    tile_expert = jnp.sum(tile[:, None] * TM >= end[None, :], axis=1).astype(jnp.int32)
    tail_start = (start + tot).astype(jnp.int32)
    tail_len = (end.at[-1].set(n_tiles * TM) - tail_start).astype(jnp.int32)

    runs = runs[:, :, :RUN_FIELDS].at[:, :, 0].add(start.astype(jnp.int32)[None, :])
    runs_flat = runs.reshape(-1)
    xs = _dispatch(tail_start, tail_len, runs_flat, loc, h2, n_tiles * TM)
    y = _experts(tile_expert, n_active.reshape(1).astype(jnp.int32), xs, w1, w3, w2)
    return _combine(runs_flat, loc, x, route, y, out_rows)


def _tile_edges(seq_lens):
    first, last = [], []
    for n in seq_lens:
        t = n // TM
        first += [1] + [0] * (t - 1)
        last += [0] * (t - 1) + [1]
    return jnp.asarray(first, jnp.int32), jnp.asarray(last, jnp.int32)


def kernel(x_prompt, x_sample, norm1_g, w_in, b_gate, conv_w, conv_b, conv_ln_g, conv_ln_b,
           q_norm_g, k_norm_g, rpb, w_proj_a, w_proj_b, w_out, norm2_g, ffn_w1, ffn_w3, ffn_w2,
           router_w, router_b, moe_w1, moe_w3, moe_w2):
    depth = w_in.shape[0]
    seq_lens = [x_prompt.shape[1]] * x_prompt.shape[0] + [x_sample.shape[1]] * x_sample.shape[0]
    assert all(n % TM == 0 and n // GRID_W >= 2 * TILE_ROWS for n in seq_lens)
    first, last = _tile_edges(seq_lens)
    x_parts = [x_prompt.reshape(-1, D_MODEL), x_sample.reshape(-1, D_MODEL)]
    out_rows = [p.shape[0] for p in x_parts]

    head_id = jnp.arange(D_ATTN) // HEAD_DIM
    head_mean = jnp.where(head_id[:, None] == head_id[None, :], 1.0 / HEAD_DIM, 0.0).astype(BF16)
    row = lambda a: a.reshape(1, -1).astype(F32)

    moe_bf16 = {}
    for l in range(depth):
        final = out_rows if l == depth - 1 else None
        j = l // 2
        moe = l % 2 == 1
        feeds_moe = not moe and l + 1 < depth
        table = _bias_table(rpb[l])
        side_w = moe_w2[j] if moe else (moe_w1[j] if feeds_moe else None)
        (vglu, q, k, v, ga, gb), side = _in_proj(
            x_parts, row(norm1_g[l]), _to_bf16(w_in, l), row(b_gate[l]),
            row(jnp.tile(q_norm_g[l], N_HEADS)), row(jnp.tile(k_norm_g[l], N_HEADS)), head_mean,
            side_w=side_w)
        if side is not None:
            moe_bf16["w2" if moe else "w1", j] = side
        if moe:
            rw = jnp.pad(router_w[j], ((0, 0), (0, LANES - N_EXPERTS))).astype(BF16)
            rb = jnp.pad(row(router_b[j]), ((0, 0), (0, LANES - N_EXPERTS)))
        else:
            rw = jnp.zeros((D_MODEL, LANES), BF16)
            rb = jnp.zeros((1, LANES), F32)
        cw = jnp.pad(conv_w[l], ((0, 1), (0, 0)))
        outs = _mixer(
            x_parts, vglu, q, k, v, ga, gb, first, last, cw, row(conv_b[l]), row(conv_ln_g[l]),
            row(conv_ln_b[l]), table, _to_bf16(w_proj_a, l), _to_bf16(w_proj_b, l),
            _to_bf16(w_out, l), row(norm2_g[l]), rw, rb, with_router=moe)
        if moe:
            x, h2, route, loc, runs, totals = outs
            w1 = moe_bf16.get(("w1", j))
            w3 = moe_bf16.get(("w3", j))
            res = _moe(h2, x, route, loc, runs, totals,
                       _to_bf16(moe_w1[j]) if w1 is None else w1,
                       _to_bf16(moe_w3[j]) if w3 is None else w3,
                       moe_bf16["w2", j], final)
        else:
            x, h2 = outs
            res, side = _ffn(h2, x, _to_bf16(ffn_w1, j), _to_bf16(ffn_w3, j),
                             _to_bf16(ffn_w2, j), final,
                             side_w=moe_w3[j] if feeds_moe else None)
            if side is not None:
                moe_bf16["w3", j] = side
        x_parts = list(res)

    return (x_parts[0].reshape(x_prompt.shape), x_parts[1].reshape(x_sample.shape))
```

```python
import functools

import jax
import jax.numpy as jnp
from jax import lax
from jax.experimental import pallas as pl
from jax.experimental.pallas import tpu as pltpu

D_MODEL = 1024
GRID_W = 64
D_CONV = 512
CONV_W = 31
N_HEADS = 8
HEAD_DIM = 64
D_ATTN = N_HEADS * HEAD_DIM
MAX_WIN_H = 8
WIN_W = 16
D_FF = 2816
N_EXPERTS = 8
TOP_K = 2
EPS = 1e-6

LANES = 128
SUBLANES = 8
BF16_ROWS = 16
TM = 512
TILE_ROWS = TM // GRID_W
KV_HALO = 256
CONV_HALO = 16
CONV_PAD = CONV_W // 2
CONV_ROWS = 256
N_RO = 2 * MAX_WIN_H - 1
N_CO = 2 * WIN_W - 1
HEAD_GROUP = 4
ROUTE_W = 8
ATT_UNROLL = 8
NEG_BIG = -1e30
LOG2E = 1.4426950408889634
FF_CHUNK = 256
VMEM_LIMIT = 56 * 1024 * 1024

F32 = jnp.float32
BF16 = jnp.bfloat16


def _sigmoid(x):
    return 1.0 / (1.0 + jnp.exp(-x))


def _round_up_rows(count):
    return jnp.floor((count + (SUBLANES - 1)) * (1.0 / SUBLANES)) * SUBLANES


def _part_specs(parts, width=D_MODEL):
    starts, specs, tile0 = [], [], 0
    for a in parts:
        n = a.shape[0] // TM
        specs.append(pl.BlockSpec(
            (TM, width), lambda i, *_, s=tile0, n=n: (jnp.clip(i - s, 0, n - 1), 0)))
        starts.append(tile0)
        tile0 += n
    return tuple(starts), specs, tile0


def _read_parts(refs, starts):
    i = pl.program_id(0)
    x = refs[0][...]
    for ref, s in zip(refs[1:], starts[1:]):
        x = jnp.where(i >= s, ref[...], x)
    return x


def _write_parts(refs, starts, value):
    i = pl.program_id(0)
    ends = starts[1:] + (None,)
    for ref, s, e in zip(refs, starts, ends):
        owns = i >= s if e is None else (i >= s) & (i < e)

        @pl.when(owns)
        def _():
            ref[...] = value


CAST_ROWS = 512
CAST_BLOCK_ELEMS = 3 * 1024 * 1024


def _cast_kernel(w_ref, o_ref):
    o_ref[...] = w_ref[...].astype(BF16)


def _to_bf16(w, take=None):
    lead = w.shape[:-2]
    k, n = w.shape[-2:]
    w3 = w.reshape((-1, k, n))
    n_mat = w3.shape[0] if take is None else 1
    first = 0 if take is None else take
    rows = k if k * n <= CAST_BLOCK_ELEMS else next(
        (r for r in (CAST_ROWS, CAST_ROWS // 2, CAST_ROWS // 4) if k % r == 0), k)
    out = pl.pallas_call(
        _cast_kernel,
        grid=(n_mat, k // rows),
        in_specs=[pl.BlockSpec((1, rows, n), lambda e, r: (first + e, r, 0))],
        out_specs=pl.BlockSpec((1, rows, n), lambda e, r: (e, r, 0)),
        out_shape=jax.ShapeDtypeStruct((n_mat, k, n), BF16),
        compiler_params=pltpu.CompilerParams(
            dimension_semantics=("arbitrary", "arbitrary"), vmem_limit_bytes=VMEM_LIMIT),
        name="to_bf16",
    )(w3)
    return out.reshape((k, n) if take is not None else lead + (k, n))


def _side_cast(w, n_steps):
    w2 = w.reshape(-1, w.shape[-1])
    rows = w2.shape[0] // n_steps
    if w2.shape[0] % n_steps or rows % BF16_ROWS:
        return None
    spec = pl.BlockSpec((rows, w2.shape[1]), lambda i, *_: (i, 0))
    return w2, spec, jax.ShapeDtypeStruct(w2.shape, BF16)


def _bias_table_kernel(rpb_ref, out_ref):
    ro = pl.program_id(0)
    shape = (GRID_W, HEAD_GROUP * GRID_W)
    kcol = lax.broadcasted_iota(jnp.int32, shape, 0)
    lane = lax.broadcasted_iota(jnp.int32, shape, 1)
    head_local = lane // GRID_W
    qcol = lane - head_local * GRID_W
    start_c = jnp.clip(qcol - WIN_W // 2, 0, GRID_W - WIN_W)
    valid = (kcol >= start_c) & (kcol < start_c + WIN_W)
    co = jnp.clip(kcol - qcol + (WIN_W - 1), 0, 2 * WIN_W - 2)
    for g in range(N_HEADS // HEAD_GROUP):
        acc = jnp.zeros(shape, F32)
        for d in range(N_CO):
            val = jnp.zeros(shape, F32)
            for hl in range(HEAD_GROUP):
                h = g * HEAD_GROUP + hl
                val = jnp.where(head_local == hl, rpb_ref[(h * N_RO + ro) * N_CO + d], val)
            acc = jnp.where(co == d, val, acc)
        out_ref[0, g] = jnp.where(valid, acc * LOG2E, NEG_BIG)


def _bias_table(rpb_l):
    n_groups = N_HEADS // HEAD_GROUP
    width = HEAD_GROUP * GRID_W
    return pl.pallas_call(
        _bias_table_kernel,
        grid=(N_RO,),
        in_specs=[pl.BlockSpec(memory_space=pltpu.SMEM)],
        out_specs=pl.BlockSpec((1, n_groups, GRID_W, width), lambda r: (r, 0, 0, 0)),
        out_shape=jax.ShapeDtypeStruct((N_RO, n_groups, GRID_W, width), F32),
        name="bias_table",
    )(rpb_l.reshape(-1))


def _in_proj_kernel(*refs, x_starts, side):
    n_parts = len(x_starts)
    g_ref, w_ref, bg_ref, qg_ref, kg_ref, hm_ref = refs[n_parts:n_parts + 6]
    outs = refs[n_parts + 6 + side:]
    vglu_ref, q_ref, k_ref, v_ref, ga_ref, gb_ref = outs[:6]
    if side:
        outs[6][...] = refs[n_parts + 6][...].astype(BF16)
    x = _read_parts(refs[:n_parts], x_starts)
    ms = jnp.mean(x * x, axis=-1, keepdims=True)
    h = (x * lax.rsqrt(ms + EPS) * g_ref[...]).astype(BF16)

    def proj(lo, hi):
        return jnp.dot(h, w_ref[:, lo:hi], preferred_element_type=F32)

    def head_norm(t, gain):
        msq = jnp.dot((t * t).astype(BF16), hm_ref[...], preferred_element_type=F32)
        return t * lax.rsqrt(msq + EPS) * gain

    a = proj(0, D_CONV)
    gate = proj(D_CONV, 2 * D_CONV)
    vglu_ref[...] = (a * _sigmoid(gate)).astype(BF16)
    o = 2 * D_CONV
    q_ref[...] = (head_norm(proj(o, o + D_ATTN), qg_ref[...])
                  * (HEAD_DIM ** -0.5 * LOG2E)).astype(BF16)
    k_ref[...] = head_norm(proj(o + D_ATTN, o + 2 * D_ATTN), kg_ref[...]).astype(BF16)
    v_ref[...] = proj(o + 2 * D_ATTN, o + 3 * D_ATTN).astype(BF16)
    o += 3 * D_ATTN
    bg = bg_ref[...]
    ga_ref[...] = _sigmoid(proj(o, o + D_MODEL) + bg[:, :D_MODEL]).astype(BF16)
    gb_ref[...] = _sigmoid(proj(o + D_MODEL, o + 2 * D_MODEL) + bg[:, D_MODEL:]).astype(BF16)


def _in_proj(x_parts, norm_g, w_in, b_gate, q_norm_g, k_norm_g, head_mean, side_w=None):
    x_starts, x_specs, n_tiles = _part_specs(x_parts)
    n_tok = n_tiles * TM
    d_in = w_in.shape[1]
    full = lambda shape: pl.BlockSpec(shape, lambda i: (0, 0))
    tile = lambda width: pl.BlockSpec((TM, width), lambda i: (i, 0))
    out = lambda width: jax.ShapeDtypeStruct((n_tok, width), BF16)
    plan = _side_cast(side_w, n_tiles) if side_w is not None else None
    side_args, side_in, side_out, side_shape = [], [], [], []
    if plan is not None:
        side_args, side_in, side_out, side_shape = [plan[0]], [plan[1]], [plan[1]], [plan[2]]
    res = pl.pallas_call(
        functools.partial(_in_proj_kernel, x_starts=x_starts, side=len(side_args)),
        grid=(n_tiles,),
        in_specs=x_specs + [full((1, D_MODEL)), full((D_MODEL, d_in)),
                            full((1, 2 * D_MODEL)), full((1, D_ATTN)), full((1, D_ATTN)),
                            full((D_ATTN, D_ATTN))] + side_in,
        out_specs=[tile(D_CONV), tile(D_ATTN), tile(D_ATTN), tile(D_ATTN),
                   tile(D_MODEL), tile(D_MODEL)] + side_out,
        out_shape=[out(D_CONV), out(D_ATTN), out(D_ATTN), out(D_ATTN),
                   out(D_MODEL), out(D_MODEL)] + side_shape,
        compiler_params=pltpu.CompilerParams(
            dimension_semantics=("arbitrary",), vmem_limit_bytes=VMEM_LIMIT),
        name="in_proj",
    )(*x_parts, norm_g, w_in, b_gate, q_norm_g, k_norm_g, head_mean, *side_args)
    side_bf16 = None
    if side_w is not None:
        side_bf16 = res[6].reshape(side_w.shape) if plan is not None else _to_bf16(side_w)
    return res[:6], side_bf16


def _mixer_kernel(first_ref, last_ref, *refs, x_starts, with_router):
    n_parts = len(x_starts)
    (vg_ref, vgp_ref, vgn_ref, q_ref,
     k_ref, kp_ref, kn_ref, v_ref, vp_ref, vn_ref,
     ga_ref, gb_ref, cw_ref, cb_ref, lg_ref, lb_ref, tab_ref,
     wa_ref, wb_ref, wo_ref, n2_ref, rw_ref, rb_ref) = refs[n_parts:n_parts + 23]
    rest = refs[n_parts + 23:]
    if with_router:
        xo_ref, h2_ref, route_ref, loc_ref, runs_ref, tot_ref = rest[:6]
    else:
        xo_ref, h2_ref = rest[:2]
    ext_ref, sh_ref, conv_ref, kext_ref, vext_ref, att_ref = rest[-6:]
    i = pl.program_id(0)
    is_first = first_ref[i]
    is_last = last_ref[i]

    keep_prev = (1 - is_first).astype(F32)
    keep_next = (1 - is_last).astype(F32)
    ext_ref[0:CONV_HALO, :] = vgp_ref[...].astype(F32) * keep_prev
    ext_ref[CONV_HALO:CONV_HALO + TM, :] = vg_ref[...].astype(F32)
    ext_ref[CONV_HALO + TM:, :] = vgn_ref[...].astype(F32) * keep_next
    n_sh_rows = sh_ref.shape[1]
    row_chunk = CONV_ROWS
    for cb in range(D_CONV // LANES):
        cols = slice(cb * LANES, (cb + 1) * LANES)
        for b in range(SUBLANES):
            sh_ref[b] = ext_ref[b:b + n_sh_rows, cols]

        def conv_rows(rc, carry, cols=cols):
            base = pl.multiple_of(rc * row_chunk, row_chunk)
            acc = jnp.zeros((row_chunk, LANES), F32)
            for j in range(CONV_W):
                off = CONV_HALO - CONV_PAD + j
                a8, b = off // SUBLANES, off % SUBLANES
                acc = acc + (sh_ref[b, pl.ds(base + SUBLANES * a8, row_chunk), :]
                             * cw_ref[j:j + 1, cols])
            conv_ref[pl.ds(base, row_chunk), cols] = acc
            return carry

        lax.fori_loop(0, TM // row_chunk, conv_rows, 0)
    y = conv_ref[...] + cb_ref[...]
    mu = jnp.mean(y, axis=-1, keepdims=True)
    yc = y - mu
    var = jnp.mean(yc * yc, axis=-1, keepdims=True)
    yn = yc * lax.rsqrt(var + EPS) * lg_ref[...] + lb_ref[...]
    act = (yn * _sigmoid(yn)).astype(BF16)
    br_a = jnp.dot(act, wa_ref[...], preferred_element_type=F32)

    kext_ref[0:KV_HALO, :] = kp_ref[...]
    kext_ref[KV_HALO:KV_HALO + TM, :] = k_ref[...]
    kext_ref[KV_HALO + TM:, :] = kn_ref[...]
    vext_ref[0:KV_HALO, :] = vp_ref[...]
    vext_ref[KV_HALO:KV_HALO + TM, :] = v_ref[...]
    vext_ref[KV_HALO + TM:, :] = vn_ref[...]
    gw = HEAD_GROUP * HEAD_DIM
    lane_head = lax.broadcasted_iota(jnp.int32, (GRID_W, gw), 1) // HEAD_DIM
    win = MAX_WIN_H * GRID_W

    def row_body(j, carry):
        ws = j - MAX_WIN_H // 2
        ws = jnp.where(is_first == 1, jnp.maximum(ws, 0), ws)
        ws = jnp.where(is_last == 1, jnp.minimum(ws, 0), ws)
        ro0 = ws - j + (MAX_WIN_H - 1)
        kstart = pl.multiple_of((ws + KV_HALO // GRID_W) * GRID_W, GRID_W)
        qstart = pl.multiple_of(j * GRID_W, GRID_W)
        for g in range(N_HEADS // HEAD_GROUP):
            cols = slice(g * gw, (g + 1) * gw)
            q4 = q_ref[pl.ds(qstart, GRID_W), cols]
            qexp = jnp.concatenate(
                [jnp.where(lane_head == hl, q4, jnp.zeros_like(q4)) for hl in range(HEAD_GROUP)],
                axis=0)
            kslab = kext_ref[pl.ds(kstart, win), cols]
            st = lax.dot_general(kslab, qexp, (((1,), (1,)), ((), ())),
                                 preferred_element_type=F32)
            st = st + jnp.concatenate([tab_ref[ro0 + w, g] for w in range(MAX_WIN_H)], axis=0)
            m = jnp.max(st, axis=0, keepdims=True)
            pexp = jnp.exp2(st - m)
            inv = 1.0 / jnp.sum(pexp, axis=0, keepdims=True)
            pn = (pexp * inv).astype(BF16)
            vslab = vext_ref[pl.ds(kstart, win), cols]
            o4 = lax.dot_general(pn, vslab, (((0,), (0,)), ((), ())),
                                 preferred_element_type=F32)
            out = o4[0:GRID_W]
            for hl in range(1, HEAD_GROUP):
                out = jnp.where(lane_head == hl, o4[hl * GRID_W:(hl + 1) * GRID_W], out)
            att_ref[pl.ds(qstart, GRID_W), cols] = out
        return carry

    lax.fori_loop(0, TILE_ROWS, row_body, 0, unroll=ATT_UNROLL)
    br_b = jnp.dot(att_ref[...].astype(BF16), wb_ref[...], preferred_element_type=F32)

    merged = ga_ref[...].astype(F32) * br_a + gb_ref[...].astype(F32) * br_b
    x_new = (_read_parts(refs[:n_parts], x_starts)
             + jnp.dot(merged.astype(BF16), wo_ref[...], preferred_element_type=F32))
    xo_ref[...] = x_new
    ms = jnp.mean(x_new * x_new, axis=-1, keepdims=True)
    h2 = x_new * lax.rsqrt(ms + EPS) * n2_ref[...]
    h2_ref[...] = h2.astype(h2_ref.dtype)

    if with_router:
        logits = jnp.dot(h2.astype(BF16), rw_ref[...], preferred_element_type=F32) + rb_ref[...]
        elane = lax.broadcasted_iota(jnp.int32, logits.shape, 1).astype(F32)
        logits = jnp.where(elane < N_EXPERTS, logits, -jnp.inf)
        m1 = jnp.max(logits, axis=1, keepdims=True)
        i1 = jnp.min(jnp.where(logits == m1, elane, float(LANES)), axis=1, keepdims=True)
        others = jnp.where(elane == i1, -jnp.inf, logits)
        m2 = jnp.max(others, axis=1, keepdims=True)
        i2 = jnp.min(jnp.where(others == m2, elane, float(LANES)), axis=1, keepdims=True)
        e2 = jnp.exp(m2 - m1)
        w1 = 1.0 / (1.0 + e2)
        w2 = e2 / (1.0 + e2)
        route = jnp.where(elane == 0, i1, jnp.where(elane == 1, i2, jnp.where(
            elane == 2, w1, jnp.where(elane == 3, w2, 0.0))))
        route_ref[...] = route[:, :ROUTE_W]

        @pl.when(i == 0)
        def _():
            tot_ref[...] = jnp.zeros(tot_ref.shape, F32)

        rt = route.T[:ROUTE_W, :]
        eidx = lax.broadcasted_iota(jnp.int32, (N_EXPERTS, TM), 0).astype(F32)
        e1 = eidx == rt[0:1, :]
        e2 = eidx == rt[1:2, :]
        chosen = (e1 | e2).astype(F32)
        before = lax.broadcasted_iota(jnp.int32, (TM, TM), 0) < lax.broadcasted_iota(
            jnp.int32, (TM, TM), 1)
        rank = jnp.dot(chosen.astype(BF16), before.astype(BF16), preferred_element_type=F32)
        rows_b = jnp.broadcast_to(_round_up_rows(jnp.sum(chosen, axis=1, keepdims=True)),
                                  (N_EXPERTS, LANES))
        expert = lax.broadcasted_iota(jnp.int32, (N_EXPERTS, LANES), 0)
        incl = rows_b
        for step in (1, 2, 4):
            incl = incl + jnp.where(expert >= step, pltpu.roll(incl, step, axis=0), 0.0)
        local = incl - rows_b
        pos = rank + local[:, 0:1]
        l1 = jnp.sum(jnp.where(e1, pos, 0.0), axis=0, keepdims=True)
        l2 = jnp.sum(jnp.where(e2, pos, 0.0), axis=0, keepdims=True)
        loc_ref[0] = jnp.concatenate([l1, l2], axis=0).astype(jnp.int32)
        lane = lax.broadcasted_iota(jnp.int32, (N_EXPERTS, LANES), 1)
        runs_ref[0] = jnp.where(lane == 0, tot_ref[...], jnp.where(
            lane == 1, rows_b, local)).astype(jnp.int32)
        tot_ref[...] += rows_b


def _mixer(x_parts, vglu, q, k, v, ga, gb, first, last, conv_w, conv_b, ln_g, ln_b, table,
           w_a, w_b, w_o, norm2_g, router_w, router_b, *, with_router):
    x_starts, x_specs, n_tiles = _part_specs(x_parts)
    n_tok = n_tiles * TM
    conv_blocks = TM // CONV_HALO
    kv_blocks = TM // KV_HALO
    full2 = lambda shape: pl.BlockSpec(shape, lambda i, f, l: (0, 0))
    tile = lambda width: pl.BlockSpec((TM, width), lambda i, f, l: (i, 0))
    in_specs = x_specs + [
        tile(D_CONV),
        pl.BlockSpec((CONV_HALO, D_CONV), lambda i, f, l: (i * conv_blocks - 1 + f[i], 0)),
        pl.BlockSpec((CONV_HALO, D_CONV), lambda i, f, l: ((i + 1) * conv_blocks - l[i], 0)),
        tile(D_ATTN),
        tile(D_ATTN),
        pl.BlockSpec((KV_HALO, D_ATTN), lambda i, f, l: (i * kv_blocks - 1 + f[i], 0)),
        pl.BlockSpec((KV_HALO, D_ATTN), lambda i, f, l: ((i + 1) * kv_blocks - l[i], 0)),
        tile(D_ATTN),
        pl.BlockSpec((KV_HALO, D_ATTN), lambda i, f, l: (i * kv_blocks - 1 + f[i], 0)),
        pl.BlockSpec((KV_HALO, D_ATTN), lambda i, f, l: ((i + 1) * kv_blocks - l[i], 0)),
        tile(D_MODEL),
        tile(D_MODEL),
        full2((CONV_W + 1, D_CONV)),
        full2((1, D_CONV)), full2((1, D_CONV)), full2((1, D_CONV)),
        pl.BlockSpec(table.shape, lambda i, f, l: (0, 0, 0, 0)),
        full2((D_CONV, D_MODEL)), full2((D_ATTN, D_MODEL)), full2((D_MODEL, D_MODEL)),
        full2((1, D_MODEL)),
        full2((D_MODEL, LANES)), full2((1, LANES)),
    ]
    out_specs = [tile(D_MODEL), tile(D_MODEL)]
    out_shape = [jax.ShapeDtypeStruct((n_tok, D_MODEL), F32),
                 jax.ShapeDtypeStruct((n_tok, D_MODEL), BF16)]
    if with_router:
        out_specs += [tile(ROUTE_W),
                      pl.BlockSpec((1, TOP_K, TM), lambda i, f, l: (i, 0, 0)),
                      pl.BlockSpec((1, N_EXPERTS, LANES), lambda i, f, l: (i, 0, 0)),
                      pl.BlockSpec((N_EXPERTS, LANES), lambda i, f, l: (0, 0))]
        out_shape += [jax.ShapeDtypeStruct((n_tok, ROUTE_W), F32),
                      jax.ShapeDtypeStruct((n_tiles, TOP_K, TM), jnp.int32),
                      jax.ShapeDtypeStruct((n_tiles, N_EXPERTS, LANES), jnp.int32),
                      jax.ShapeDtypeStruct((N_EXPERTS, LANES), F32)]
    ext_rows = TM + 2 * CONV_HALO
    scratch = [
        pltpu.VMEM((ext_rows, D_CONV), F32),
        pltpu.VMEM((SUBLANES, ext_rows - SUBLANES, LANES), F32),
        pltpu.VMEM((TM, D_CONV), F32),
        pltpu.VMEM((TM + 2 * KV_HALO, D_ATTN), BF16),
        pltpu.VMEM((TM + 2 * KV_HALO, D_ATTN), BF16),
        pltpu.VMEM((TM, D_ATTN), F32),
    ]
    return pl.pallas_call(
        functools.partial(_mixer_kernel, x_starts=x_starts, with_router=with_router),
        grid_spec=pltpu.PrefetchScalarGridSpec(
            num_scalar_prefetch=2, grid=(n_tiles,),
            in_specs=in_specs, out_specs=out_specs, scratch_shapes=scratch),
        out_shape=out_shape,
        compiler_params=pltpu.CompilerParams(
            dimension_semantics=("arbitrary",), vmem_limit_bytes=VMEM_LIMIT),
        name="mixer",
    )(first, last, *x_parts, vglu, vglu, vglu, q, k, k, k, v, v, v, ga, gb,
      conv_w, conv_b, ln_g, ln_b, table, w_a, w_b, w_o, norm2_g, router_w, router_b)


def _ffn_kernel(h_ref, x_ref, w1_ref, w3_ref, w2_ref, *rest, o_starts, side):
    o_refs = rest
    if side:
        rest[-1][...] = rest[0][...].astype(BF16)
        o_refs = rest[1:-1]
    h = h_ref[...]
    acc = x_ref[...]
    for c in range(0, D_FF, FF_CHUNK):
        a = jnp.dot(h, w1_ref[:, c:c + FF_CHUNK], preferred_element_type=F32)
        b = jnp.dot(h, w3_ref[:, c:c + FF_CHUNK], preferred_element_type=F32)
        act = (a * _sigmoid(a) * b).astype(BF16)
        acc = acc + jnp.dot(act, w2_ref[c:c + FF_CHUNK, :], preferred_element_type=F32)
    _write_parts(o_refs, o_starts, acc)


def _out_parts(n_tok, out_rows):
    shapes = [jax.ShapeDtypeStruct((r, D_MODEL), F32) for r in (out_rows or [n_tok])]
    starts, specs, _ = _part_specs(shapes)
    return shapes, starts, specs


def _ffn(h2, x, w1, w3, w2, out_rows=None, side_w=None):
    n_tok = x.shape[0]
    tile = lambda: pl.BlockSpec((TM, D_MODEL), lambda i: (i, 0))
    single = dict(pipeline_mode=pl.Buffered(1))
    o_shapes, o_starts, o_specs = _out_parts(n_tok, out_rows)
    plan = _side_cast(side_w, n_tok // TM) if side_w is not None else None
    side_args, side_spec, side_shape = [], [], []
    if plan is not None:
        side_args, side_spec, side_shape = [plan[0]], [plan[1]], [plan[2]]
    res = pl.pallas_call(
        functools.partial(_ffn_kernel, o_starts=o_starts, side=len(side_args)),
        grid=(n_tok // TM,),
        in_specs=[tile(), tile(),
                  pl.BlockSpec((D_MODEL, D_FF), lambda i: (0, 0), **single),
                  pl.BlockSpec((D_MODEL, D_FF), lambda i: (0, 0), **single),
                  pl.BlockSpec((D_FF, D_MODEL), lambda i: (0, 0), **single)] + side_spec,
        out_specs=o_specs + side_spec,
        out_shape=o_shapes + side_shape,
        compiler_params=pltpu.CompilerParams(
            dimension_semantics=("arbitrary",), vmem_limit_bytes=VMEM_LIMIT),
        name="ffn",
    )(h2, x, w1, w3, w2, *side_args)
    side_bf16 = None
    if side_w is not None:
        side_bf16 = res[-1].reshape(side_w.shape) if plan is not None else _to_bf16(side_w)
    return res[:len(o_shapes)], side_bf16


RUN_FIELDS = 3
PERM_ROWS = TOP_K * TM + N_EXPERTS * SUBLANES


def _run_copies(runs_ref, tile, make_copy, wait):
    for e in range(N_EXPERTS):
        base = (tile * N_EXPERTS + e) * RUN_FIELDS
        sorted_row, n, local_row = runs_ref[base], runs_ref[base + 1], runs_ref[base + 2]

        @pl.when(n > 0)
        def _():
            copy = make_copy(pl.multiple_of(sorted_row, SUBLANES),
                             pl.multiple_of(local_row, SUBLANES), pl.multiple_of(n, SUBLANES))
            copy.wait() if wait else copy.start()


def _perm_matrix(loc_row):
    rows = lax.broadcasted_iota(jnp.int32, (PERM_ROWS, TM), 0)
    return rows == loc_row


ZERO_ROWS = 64


def _dispatch_kernel(ts_ref, tl_ref, runs_ref, loc_ref, h_ref, xs_ref,
                     zero_ref, grouped_ref, sem, zsem):
    i = pl.program_id(0)

    def fill_padding(wait):
        def run(copy):
            copy.wait() if wait else copy.start()

        for e in range(N_EXPERTS):
            base = ts_ref[e]
            n = tl_ref[e]
            full = n // ZERO_ROWS

            def chunk(c, carry):
                off = pl.multiple_of(base + c * ZERO_ROWS, SUBLANES)
                run(pltpu.make_async_copy(zero_ref, xs_ref.at[pl.ds(off, ZERO_ROWS)], zsem))
                return carry

            lax.fori_loop(0, full, chunk, 0)
            rem_base = base + full * ZERO_ROWS
            for b in range(SUBLANES.bit_length() - 1, ZERO_ROWS.bit_length() - 1):
                size = 1 << b

                @pl.when(((n >> b) & 1) == 1)
                def _():
                    off = pl.multiple_of(rem_base + (n & (size - 1)), SUBLANES)
                    run(pltpu.make_async_copy(
                        zero_ref.at[pl.ds(0, size)], xs_ref.at[pl.ds(off, size)], zsem))

    @pl.when(i == 0)
    def _():
        zero_ref[...] = jnp.zeros(zero_ref.shape, F32)
        fill_padding(wait=False)

    slot = lax.rem(i, 2)
    loc = loc_ref[0]
    perm = (_perm_matrix(loc[0:1, :]) | _perm_matrix(loc[1:2, :])).astype(BF16)
    grouped_ref[slot] = jnp.dot(perm, h_ref[...], preferred_element_type=F32)

    def run_copy(buf):
        def make(sorted_row, local_row, size):
            return pltpu.make_async_copy(grouped_ref.at[buf, pl.ds(local_row, size)],
                                         xs_ref.at[pl.ds(sorted_row, size)], sem.at[buf])
        return make

    _run_copies(runs_ref, i, run_copy(slot), wait=False)

    @pl.when(i > 0)
    def _():
        _run_copies(runs_ref, i - 1, run_copy(1 - slot), wait=True)

    @pl.when(i == pl.num_programs(0) - 1)
    def _():
        _run_copies(runs_ref, i, run_copy(slot), wait=True)

    @pl.when(i == 0)
    def _():
        fill_padding(wait=True)


def _dispatch(tail_start, tail_len, runs_flat, loc, h2, n_rows):
    n_tok = h2.shape[0]
    return pl.pallas_call(
        _dispatch_kernel,
        grid_spec=pltpu.PrefetchScalarGridSpec(
            num_scalar_prefetch=3, grid=(n_tok // TM,),
            in_specs=[pl.BlockSpec((1, TOP_K, TM), lambda i, *_: (i, 0, 0)),
                      pl.BlockSpec((TM, D_MODEL), lambda i, *_: (i, 0))],
            out_specs=pl.BlockSpec(memory_space=pl.ANY),
            scratch_shapes=[pltpu.VMEM((ZERO_ROWS, D_MODEL), F32),
                            pltpu.VMEM((2, PERM_ROWS, D_MODEL), F32),
                            pltpu.SemaphoreType.DMA((2,)), pltpu.SemaphoreType.DMA]),
        out_shape=jax.ShapeDtypeStruct((n_rows, D_MODEL), F32),
        compiler_params=pltpu.CompilerParams(
            dimension_semantics=("arbitrary",), vmem_limit_bytes=VMEM_LIMIT),
        name="moe_dispatch",
    )(tail_start, tail_len, runs_flat, loc, h2)


def _expert_kernel(te_ref, na_ref, x_ref, w1_ref, w3_ref, w2_ref, y_ref):
    j = pl.program_id(0)

    @pl.when(j < na_ref[0])
    def _():
        h = x_ref[...].astype(BF16)
        acc = jnp.zeros((TM, D_MODEL), F32)
        for c in range(0, D_FF, FF_CHUNK):
            a = jnp.dot(h, w1_ref[0, :, c:c + FF_CHUNK], preferred_element_type=F32)
            b = jnp.dot(h, w3_ref[0, :, c:c + FF_CHUNK], preferred_element_type=F32)
            act = (a * _sigmoid(a) * b).astype(BF16)
            acc = acc + jnp.dot(act, w2_ref[0, c:c + FF_CHUNK, :], preferred_element_type=F32)
        y_ref[...] = acc

    @pl.when(j >= na_ref[0])
    def _():
        y_ref[...] = jnp.zeros(y_ref.shape, F32)


def _experts(tile_expert, n_active, xs, w1, w3, w2):
    n_tiles = xs.shape[0] // TM
    weights = lambda shape: pl.BlockSpec(shape, lambda j, te, na: (te[j], 0, 0))
    return pl.pallas_call(
        _expert_kernel,
        grid_spec=pltpu.PrefetchScalarGridSpec(
            num_scalar_prefetch=2, grid=(n_tiles,),
            in_specs=[pl.BlockSpec((TM, D_MODEL),
                                   lambda j, te, na: (jnp.minimum(j, na[0] - 1), 0)),
                      weights((1, D_MODEL, D_FF)), weights((1, D_MODEL, D_FF)),
                      weights((1, D_FF, D_MODEL))],
            out_specs=pl.BlockSpec((TM, D_MODEL), lambda j, te, na: (j, 0))),
        out_shape=jax.ShapeDtypeStruct(xs.shape, F32),
        compiler_params=pltpu.CompilerParams(
            dimension_semantics=("arbitrary",), vmem_limit_bytes=VMEM_LIMIT),
        name="moe_experts",
    )(tile_expert, n_active, xs, w1, w3, w2)


def _combine_kernel(runs_ref, loc_ref, x_ref, route_ref, y_ref, *refs, o_starts):
    stage_ref, sem = refs[-2:]
    i = pl.program_id(0)
    slot = lax.rem(i, 2)

    def run_copy(buf):
        def make(sorted_row, local_row, size):
            return pltpu.make_async_copy(y_ref.at[pl.ds(sorted_row, size)],
                                         stage_ref.at[buf, pl.ds(local_row, size)], sem.at[buf])
        return make

    @pl.when(i == 0)
    def _():
        stage_ref[...] = jnp.zeros(stage_ref.shape, F32)
        _run_copies(runs_ref, i, run_copy(slot), wait=False)

    @pl.when(i + 1 < pl.num_programs(0))
    def _():
        _run_copies(runs_ref, i + 1, run_copy(1 - slot), wait=False)

    _run_copies(runs_ref, i, run_copy(slot), wait=True)
    grouped = stage_ref[slot].astype(BF16)
    loc = loc_ref[0]
    route = route_ref[...]
    out = x_ref[...]
    for c in range(TOP_K):
        picked = lax.dot_general(_perm_matrix(loc[c:c + 1, :]).astype(BF16), grouped,
                                 (((0,), (0,)), ((), ())), preferred_element_type=F32)
        out = out + route[:, TOP_K + c:TOP_K + c + 1] * picked
    _write_parts(refs[:-2], o_starts, out)


def _combine(runs_flat, loc, x, route, y, out_rows=None):
    n_tok = x.shape[0]
    o_shapes, o_starts, o_specs = _out_parts(n_tok, out_rows)
    return pl.pallas_call(
        functools.partial(_combine_kernel, o_starts=o_starts),
        grid_spec=pltpu.PrefetchScalarGridSpec(
            num_scalar_prefetch=1, grid=(n_tok // TM,),
            in_specs=[pl.BlockSpec((1, TOP_K, TM), lambda i, *_: (i, 0, 0)),
                      pl.BlockSpec((TM, D_MODEL), lambda i, *_: (i, 0)),
                      pl.BlockSpec((TM, ROUTE_W), lambda i, *_: (i, 0)),
                      pl.BlockSpec(memory_space=pl.ANY)],
            out_specs=o_specs,
            scratch_shapes=[pltpu.VMEM((2, PERM_ROWS, D_MODEL), F32),
                            pltpu.SemaphoreType.DMA((2,))]),
        out_shape=o_shapes,
        compiler_params=pltpu.CompilerParams(
            dimension_semantics=("arbitrary",), vmem_limit_bytes=VMEM_LIMIT),
        name="moe_combine",
    )(runs_flat, loc, x, route, y)


def _moe(h2, x, route, loc, runs, totals, w1, w3, w2, out_rows=None):
    n_tok = x.shape[0]
    tot = totals[:, 0].astype(jnp.int32)
    padded = (tot + TM - 1) // TM * TM
    end = jnp.cumsum(padded)
    start = end - padded
    max_rows = TOP_K * n_tok + (n_tok // TM) * N_EXPERTS * (SUBLANES - 1) + N_EXPERTS * (TM - 1)
    n_tiles = max_rows // TM
    n_active = end[-1] // TM
    tile = jnp.minimum(jnp.arange(n_tiles, dtype=jnp.int32), n_active - 1)
    tile_expert = jnp.sum(tile[:, None] * TM >= end[None, :], axis=1).astype(jnp.int32)
    tail_start = (start + tot).astype(jnp.int32)
    tail_len = (end.at[-1].set(n_tiles * TM) - tail_start).astype(jnp.int32)

    runs = runs[:, :, :RUN_FIELDS].at[:, :, 0].add(start.astype(jnp.int32)[None, :])
    runs_flat = runs.reshape(-1)
    xs = _dispatch(tail_start, tail_len, runs_flat, loc, h2, n_tiles * TM)
    y = _experts(tile_expert, n_active.reshape(1).astype(jnp.int32), xs, w1, w3, w2)
    return _combine(runs_flat, loc, x, route, y, out_rows)


def _tile_edges(seq_lens):
    first, last = [], []
    for n in seq_lens:
        t = n // TM
        first += [1] + [0] * (t - 1)
        last += [0] * (t - 1) + [1]
    return jnp.asarray(first, jnp.int32), jnp.asarray(last, jnp.int32)


def kernel(x_prompt, x_sample, norm1_g, w_in, b_gate, conv_w, conv_b, conv_ln_g, conv_ln_b,
           q_norm_g, k_norm_g, rpb, w_proj_a, w_proj_b, w_out, norm2_g, ffn_w1, ffn_w3, ffn_w2,
           router_w, router_b, moe_w1, moe_w3, moe_w2):
    depth = w_in.shape[0]
    seq_lens = [x_prompt.shape[1]] * x_prompt.shape[0] + [x_sample.shape[1]] * x_sample.shape[0]
    assert all(n % TM == 0 and n // GRID_W >= 2 * TILE_ROWS for n in seq_lens)
    first, last = _tile_edges(seq_lens)
    x_parts = [x_prompt.reshape(-1, D_MODEL), x_sample.reshape(-1, D_MODEL)]
    out_rows = [p.shape[0] for p in x_parts]

    head_id = jnp.arange(D_ATTN) // HEAD_DIM
    head_mean = jnp.where(head_id[:, None] == head_id[None, :], 1.0 / HEAD_DIM, 0.0).astype(BF16)
    row = lambda a: a.reshape(1, -1).astype(F32)

    moe_bf16 = {}
    for l in range(depth):
        final = out_rows if l == depth - 1 else None
        j = l // 2
        moe = l % 2 == 1
        feeds_moe = not moe and l + 1 < depth
        table = _bias_table(rpb[l])
        side_w = moe_w2[j] if moe else (moe_w1[j] if feeds_moe else None)
        (vglu, q, k, v, ga, gb), side = _in_proj(
            x_parts, row(norm1_g[l]), _to_bf16(w_in, l), row(b_gate[l]),
            row(jnp.tile(q_norm_g[l], N_HEADS)), row(jnp.tile(k_norm_g[l], N_HEADS)), head_mean,
            side_w=side_w)
        if side is not None:
            moe_bf16["w2" if moe else "w1", j] = side
        if moe:
            rw = jnp.pad(router_w[j], ((0, 0), (0, LANES - N_EXPERTS))).astype(BF16)
            rb = jnp.pad(row(router_b[j]), ((0, 0), (0, LANES - N_EXPERTS)))
        else:
            rw = jnp.zeros((D_MODEL, LANES), BF16)
            rb = jnp.zeros((1, LANES), F32)
        cw = jnp.pad(conv_w[l], ((0, 1), (0, 0)))
        outs = _mixer(
            x_parts, vglu, q, k, v, ga, gb, first, last, cw, row(conv_b[l]), row(conv_ln_g[l]),
            row(conv_ln_b[l]), table, _to_bf16(w_proj_a, l), _to_bf16(w_proj_b, l),
            _to_bf16(w_out, l), row(norm2_g[l]), rw, rb, with_router=moe)
        if moe:
            x, h2, route, loc, runs, totals = outs
            w1 = moe_bf16.get(("w1", j))
            w3 = moe_bf16.get(("w3", j))
            res = _moe(h2, x, route, loc, runs, totals,
                       _to_bf16(moe_w1[j]) if w1 is None else w1,
                       _to_bf16(moe_w3[j]) if w3 is None else w3,
                       moe_bf16["w2", j], final)
        else:
            x, h2 = outs
            res, side = _ffn(h2, x, _to_bf16(ffn_w1, j), _to_bf16(ffn_w3, j),
                             _to_bf16(ffn_w2, j), final,
                             side_w=moe_w3[j] if feeds_moe else None)
            if side is not None:
                moe_bf16["w3", j] = side
        x_parts = list(res)

    return (x_parts[0].reshape(x_prompt.shape), x_parts[1].reshape(x_sample.shape))
```
